```python
import math
import jax
import jax.numpy as jnp
from jax import lax
import numpy as np

D_MODEL = 2048
BATCH = 8
SEQ = 2048
DEPTH = 2

BRANCH_WIDTH = D_MODEL // 2
MIX_WIDTH = 3 * BRANCH_WIDTH
NORM_EPS = 1e-6

A_QK_DIM = 64
A_V_DIM = 2 * A_QK_DIM
A_HEADS = BRANCH_WIDTH // A_V_DIM
A_SUBLN_EPS = 1e-5
Q_BLOCK = 128
NUM_BUCKETS = 32
MAX_DISTANCE = 128

B_HEAD_SIZE = 64
B_HEADS = BRANCH_WIDTH // B_HEAD_SIZE
B_LORA = 64
B_LNX_EPS = 64e-5
B_SHIFT_WIDTH = 3 * BRANCH_WIDTH + 4 * B_LORA

C_DK = 128
C_DV = 128
C_HEADS = BRANCH_WIDTH // C_DV
C_CHUNK = 32
LB_FLOOR = 1e-30

PROJ_SIZES = (
    A_HEADS * 2 * A_QK_DIM,
    A_HEADS * 2 * A_QK_DIM,
    A_HEADS * A_V_DIM,
    BRANCH_WIDTH,
    B_SHIFT_WIDTH,
    BRANCH_WIDTH,
    C_HEADS * C_DK,
    C_HEADS * C_DV,
    2 * C_HEADS * C_DK,
    BRANCH_WIDTH,
)
PROJ_WIDTH = sum(PROJ_SIZES)

kernel_name = "hybrid_diffattn_rwkv7_hgrn2_encoder"


def _split(t, sizes):
    parts, start = [], 0
    for size in sizes:
        parts.append(t[..., start:start + size])
        start += size
    return parts


def rmsnorm(x, g, eps=NORM_EPS):
    xf = x.astype(jnp.float32)
    y = xf * lax.rsqrt(jnp.mean(xf * xf, axis=-1, keepdims=True) + eps)
    return (y * g.astype(jnp.float32)).astype(x.dtype)


def t5_bucket(rel):
    half = NUM_BUCKETS // 2
    max_exact = half // 2
    n = jnp.abs(rel)
    nf = jnp.maximum(n, max_exact).astype(jnp.float32)
    large = max_exact + (jnp.log(nf / max_exact) / math.log(MAX_DISTANCE / max_exact)
                         * (half - max_exact)).astype(jnp.int32)
    large = jnp.minimum(large, half - 1)
    return jnp.where(rel > 0, half, 0) + jnp.where(n < max_exact, n, large)


def diff_attention(q, k, v, rel_bias, lam_q1, lam_k1, lam_q2, lam_k2, subln_g, layer_idx):
    f32 = jnp.float32
    bsz, seq = q.shape[0], q.shape[1]
    n_blocks = seq // Q_BLOCK
    lambda_init = 0.8 - 0.6 * math.exp(-0.3 * layer_idx)
    lam = (jnp.exp(jnp.sum(lam_q1.astype(f32) * lam_k1.astype(f32)))
           - jnp.exp(jnp.sum(lam_q2.astype(f32) * lam_k2.astype(f32))) + lambda_init)
    scale = A_QK_DIM ** -0.5
    key_pos = jnp.arange(seq, dtype=jnp.int32)
    q_blocks = q.reshape(bsz, n_blocks, Q_BLOCK, A_HEADS, 2, A_QK_DIM).swapaxes(0, 1)

    def one_block(args):
        q_blk, blk = args
        q_pos = blk * Q_BLOCK + jnp.arange(Q_BLOCK, dtype=jnp.int32)
        bias = rel_bias[t5_bucket(key_pos[None, :] - q_pos[:, None])]
        bias = jnp.transpose(bias, (2, 0, 1)).astype(f32)
        logits = jnp.einsum('bqhmd,bkhmd->bhmqk', q_blk, k).astype(f32) * scale
        probs = jax.nn.softmax(logits + bias[None, :, None], axis=-1)
        weights = probs[:, :, 0] - lam * probs[:, :, 1]
        return jnp.einsum('bhqk,bkhd->bqhd', weights.astype(v.dtype), v)

    out = lax.map(one_block, (q_blocks, jnp.arange(n_blocks, dtype=jnp.int32)))
    out = out.swapaxes(0, 1).reshape(bsz, seq, A_HEADS, A_V_DIM)
    out = rmsnorm(out, subln_g, eps=A_SUBLN_EPS) * (1.0 - lambda_init)
    return out.reshape(bsz, seq, A_HEADS * A_V_DIM)


def centred_shift(p, mu):
    prev = jnp.pad(p, ((0, 0), (1, 0), (0, 0)))[:, :-1]
    nxt = jnp.pad(p, ((0, 0), (0, 1), (0, 0)))[:, 1:]
    return p + mu[0] * (prev - p) + mu[1] * (nxt - p)


def _directional(t):
    bsz, seq = t.shape[0], t.shape[1]
    both = jnp.stack([t[:, :, 0], jnp.flip(t[:, :, 1], axis=1)], axis=0)
    return both.reshape(2, bsz, seq, B_HEADS, B_HEAD_SIZE).transpose(2, 0, 1, 3, 4)


def rwkv7_step(state, inp):
    r, w, kk, kk_a, v, k = inp
    sa = jnp.einsum('dbhij,dbhj->dbhi', state, -kk)
    state = (state * w[..., None, :] + sa[..., :, None] * kk_a[..., None, :]
             + v[..., :, None] * k[..., None, :])
    return state, jnp.einsum('dbhij,dbhj->dbhi', state, r)


def rwkv7_bidir(r, k, v, w_down, a_down, w0, w_up, a0, a_up, k_k, k_a, r_k, lnx_g, lnx_b):
    f32 = jnp.float32
    bsz, seq, width = r.shape
    r, k, v = r.astype(f32), k.astype(f32), v.astype(f32)
    decay = jnp.exp(-math.exp(-0.5) * jax.nn.sigmoid(
        w0.astype(f32) + jnp.einsum('bsdl,dlc->bsdc', jnp.tanh(w_down.astype(f32)), w_up.astype(f32))))
    a = jax.nn.sigmoid(a0.astype(f32) + jnp.einsum('bsdl,dlc->bsdc', a_down.astype(f32), a_up.astype(f32)))
    kk = (k * k_k.astype(f32)).reshape(bsz, seq, B_HEADS, B_HEAD_SIZE)
    kk = kk / jnp.maximum(jnp.linalg.norm(kk, axis=-1, keepdims=True), 1e-12)
    kk = kk.reshape(bsz, seq, 1, width)
    k_mod = k[:, :, None] * (1.0 + (a - 1.0) * k_a.astype(f32))
    both = lambda t: jnp.broadcast_to(t, (bsz, seq, 2, width))
    xs = (_directional(both(r[:, :, None])), _directional(decay), _directional(both(kk)),
          _directional(kk * a), _directional(both(v[:, :, None])), _directional(k_mod))
    state0 = jnp.zeros((2, bsz, B_HEADS, B_HEAD_SIZE, B_HEAD_SIZE), f32)
    _, ys = lax.scan(rwkv7_step, state0, xs)
    y = (ys[:, 0] + jnp.flip(ys[:, 1], axis=0)).transpose(1, 0, 2, 3)
    mu = jnp.mean(y, axis=-1, keepdims=True)
    var = jnp.mean(jnp.square(y - mu), axis=-1, keepdims=True)
    y = ((y - mu) * lax.rsqrt(var + B_LNX_EPS)).reshape(bsz, seq, width)
    y = y * lnx_g.astype(f32) + lnx_b.astype(f32)
    bonus = jnp.einsum('bshn,bsdhn,hn->bsh', r.reshape(bsz, seq, B_HEADS, B_HEAD_SIZE),
                       k_mod.reshape(bsz, seq, 2, B_HEADS, B_HEAD_SIZE),
                       r_k.astype(f32).reshape(B_HEADS, B_HEAD_SIZE))
    return y + (bonus[..., None] * v.reshape(bsz, seq, B_HEADS, B_HEAD_SIZE)).reshape(bsz, seq, width)


def gla_chunkwise(q, k, v, log_f):
    nbat, heads, seq, dk = q.shape
    dv = v.shape[-1]
    n_chunks = seq // C_CHUNK
    to_chunks = lambda t: t.reshape(nbat, heads, n_chunks, C_CHUNK, t.shape[-1]).transpose(2, 0, 1, 3, 4)
    mask = jnp.tril(jnp.ones((C_CHUNK, C_CHUNK), dtype=bool))[:, :, None]

    def step(state, inp):
        q_c, k_c, v_c, lf_c = inp
        cum = jnp.cumsum(lf_c, axis=-2)
        diff = cum[..., :, None, :] - cum[..., None, :, :]
        pair_decay = jnp.where(mask, jnp.exp(jnp.where(mask, diff, 0.0)), 0.0)
        scores = jnp.einsum('bhtk,bhsk,bhtsk->bhts', q_c, k_c, pair_decay)
        o_c = (jnp.einsum('bhts,bhsv->bhtv', scores, v_c)
               + jnp.einsum('bhtk,bhkv->bhtv', q_c * jnp.exp(cum), state))
        last = cum[..., -1:, :]
        state = (state * jnp.exp(last)[..., 0, :, None]
                 + jnp.einsum('bhsk,bhsv->bhkv', k_c * jnp.exp(last - cum), v_c))
        return state, o_c

    state0 = jnp.zeros((nbat, heads, dk, dv), q.dtype)
    _, o = lax.scan(step, state0, (to_chunks(q), to_chunks(k), to_chunks(v), to_chunks(log_f)))
    return o.transpose(1, 2, 0, 3, 4).reshape(nbat, heads, seq, dv)


def hgrn2_bidir(q, i, f_logits, lower_bound, norm_g):
    f32 = jnp.float32
    bsz, seq = q.shape[0], q.shape[1]
    z = f_logits.astype(f32)
    log_sig = jax.nn.log_sigmoid(z)
    log_lb = jnp.log(jnp.maximum(lower_bound, LB_FLOOR))
    log_f = jnp.where(lower_bound > 0.0,
                      jnp.logaddexp(log_lb, jnp.log1p(-lower_bound) + log_sig), log_sig)
    k = (1.0 - lower_bound) * jax.nn.sigmoid(-z)
    heads = lambda t, d: t.reshape(t.shape[0], seq, C_HEADS, d).transpose(0, 2, 1, 3)
    both = lambda t_f, t_b: jnp.concatenate([t_f, jnp.flip(t_b, axis=1)], axis=0)
    qf, vf = q.astype(f32), i.astype(f32)
    o = gla_chunkwise(heads(both(qf, qf), C_DK), heads(both(k[:, :, 0], k[:, :, 1]), C_DK),
                      heads(both(vf, vf), C_DV), heads(both(log_f[:, :, 0], log_f[:, :, 1]), C_DK))
    o = o.transpose(0, 2, 1, 3)
    o = o[:bsz] + jnp.flip(o[bsz:], axis=1)
    return rmsnorm(o, norm_g).reshape(bsz, seq, C_HEADS * C_DV)


def setup_inputs(seed: int = 0) -> dict:
    key = jax.random.key(seed)
    ks = iter(jax.random.split(key, 24))
    f32 = jnp.float32
    nrm = lambda shape, scale: scale * jax.random.normal(next(ks), shape, f32)
    gain = lambda shape: 1.0 + 0.05 * jax.random.normal(next(ks), shape, f32)
    return {
        "x": nrm((BATCH, SEQ, D_MODEL), 1.0),
        "rel_bias": nrm((NUM_BUCKETS, A_HEADS), 0.5),
        "pre_norm_g": gain((DEPTH, D_MODEL)),
        "post_norm_g": gain((DEPTH, D_MODEL)),
        "w_in": nrm((DEPTH, D_MODEL, PROJ_WIDTH), D_MODEL ** -0.5),
        "w_out": nrm((DEPTH, MIX_WIDTH, D_MODEL), MIX_WIDTH ** -0.5),
        "lambda_q1": nrm((DEPTH, A_QK_DIM), 0.1),
        "lambda_k1": nrm((DEPTH, A_QK_DIM), 0.1),
        "lambda_q2": nrm((DEPTH, A_QK_DIM), 0.1),
        "lambda_k2": nrm((DEPTH, A_QK_DIM), 0.1),
        "subln_g": gain((DEPTH, A_V_DIM)),
        "rwkv_shift_mu": jax.random.uniform(next(ks), (DEPTH, 2, B_SHIFT_WIDTH), f32, 0.0, 0.5),
        "rwkv_w0": jax.random.uniform(next(ks), (DEPTH, 2, BRANCH_WIDTH), f32, -2.0, 2.0),
        "rwkv_w_up": nrm((DEPTH, 2, B_LORA, BRANCH_WIDTH), 0.1),
        "rwkv_a0": nrm((DEPTH, 2, BRANCH_WIDTH), 0.5),
        "rwkv_a_up": nrm((DEPTH, 2, B_LORA, BRANCH_WIDTH), B_LORA ** -0.5),
        "rwkv_k_k": 0.85 + 0.05 * jax.random.normal(next(ks), (DEPTH, BRANCH_WIDTH), f32),
        "rwkv_k_a": gain((DEPTH, BRANCH_WIDTH)),
        "rwkv_r_k": nrm((DEPTH, BRANCH_WIDTH), 0.1),
        "rwkv_lnx_g": gain((DEPTH, BRANCH_WIDTH)),
        "rwkv_lnx_b": nrm((DEPTH, BRANCH_WIDTH), 0.01),
        "hgrn_lb_logits": nrm((2, DEPTH, C_HEADS * C_DK), 0.5),
        "hgrn_norm_g": gain((DEPTH, C_DV)),
    }


def reference(x, rel_bias, pre_norm_g, post_norm_g, w_in, w_out, lambda_q1, lambda_k1, lambda_q2,
              lambda_k2, subln_g, rwkv_shift_mu, rwkv_w0, rwkv_w_up, rwkv_a0, rwkv_a_up, rwkv_k_k,
              rwkv_k_a, rwkv_r_k, rwkv_lnx_g, rwkv_lnx_b, hgrn_lb_logits, hgrn_norm_g):
    bsz, seq, _ = x.shape
    lb = jax.nn.softmax(hgrn_lb_logits.astype(jnp.float32), axis=1)
    lb = jnp.cumsum(lb, axis=1) - lb[:, :1]
    for l in range(DEPTH):
        u = rmsnorm(x, pre_norm_g[l])
        proj = jnp.einsum('bsd,dp->bsp', u, w_in[l])
        aq, ak, av, ag, b_streams, bg, cq, ci, cf, cg = _split(proj, PROJ_SIZES)
        y_a = diff_attention(
            aq.reshape(bsz, seq, A_HEADS, 2, A_QK_DIM), ak.reshape(bsz, seq, A_HEADS, 2, A_QK_DIM),
            av.reshape(bsz, seq, A_HEADS, A_V_DIM), rel_bias, lambda_q1[l], lambda_k1[l],
            lambda_q2[l], lambda_k2[l], subln_g[l], l)
        br, bk, bv, bw, ba = _split(centred_shift(b_streams, rwkv_shift_mu[l]),
                                    (BRANCH_WIDTH, BRANCH_WIDTH, BRANCH_WIDTH, 2 * B_LORA, 2 * B_LORA))
        y_b = rwkv7_bidir(br, bk, bv, bw.reshape(bsz, seq, 2, B_LORA), ba.reshape(bsz, seq, 2, B_LORA),
                          rwkv_w0[l], rwkv_w_up[l], rwkv_a0[l], rwkv_a_up[l], rwkv_k_k[l], rwkv_k_a[l],
                          rwkv_r_k[l], rwkv_lnx_g[l], rwkv_lnx_b[l])
        y_c = hgrn2_bidir(cq, ci, cf.reshape(bsz, seq, 2, C_HEADS * C_DK), lb[:, l], hgrn_norm_g[l])
        mixed = jnp.concatenate([(y_a * jax.nn.silu(ag)).astype(u.dtype),
                                 (y_b * jax.nn.silu(bg)).astype(u.dtype),
                                 (y_c * jax.nn.silu(cg)).astype(u.dtype)], axis=-1)
        x = x + rmsnorm(jnp.einsum('bsm,md->bsd', mixed, w_out[l]), post_norm_g[l])
    return x
```

```python
import functools
import math

import numpy as np
import jax
import jax.numpy as jnp
from jax import lax
from jax.experimental import pallas as pl
from jax.experimental.pallas import tpu as pltpu

F32 = jnp.float32
BF16 = jnp.bfloat16

LANES = 128
VMEM_LIMIT = 52 * 1024 * 1024

NORM_EPS = 1e-6
A_QK = 64
A_SUBLN_EPS = 1e-5
NUM_BUCKETS = 32
MAX_DISTANCE = 128
ATT_BLOCK = 256
B_HEAD = 64
B_LORA = 64
B_LNX_EPS = 64e-5
B_CHUNK = 64
C_CHUNK = 128
LB_FLOOR = 1e-30
ROW_BLOCK = 256

_NT = (((1,), (1,)), ((), ()))


def _dot(a, b):
    return jnp.dot(a, b, preferred_element_type=F32)


def _dot_nt(a, b):
    return lax.dot_general(a, b, _NT, preferred_element_type=F32)


def _split_dot(a, b, split, passes):
    rem = a if split == 0 else b
    acc = None
    for _ in range(passes):
        piece = rem.astype(BF16)
        d = _dot(piece, b) if split == 0 else _dot(a, piece)
        acc = d if acc is None else acc + d
        rem = rem - piece.astype(F32)
    return acc


def _sigmoid(x):
    return 1.0 / (1.0 + jnp.exp(-x))


def _cparams(sem):
    return pltpu.CompilerParams(dimension_semantics=sem, vmem_limit_bytes=VMEM_LIMIT)


def _proj_body(x_ref, g_ref, w_ref, o_ref, u_s):
    @pl.when(pl.program_id(1) == 0)
    def _():
        xf = x_ref[...]
        ms = jnp.mean(xf * xf, axis=-1, keepdims=True)
        u_s[...] = (xf * lax.rsqrt(ms + NORM_EPS) * g_ref[...]).astype(BF16)

    o_ref[...] = _dot(u_s[...], w_ref[...])


def _proj(x2, g, w, tm=512, tn=1024):
    t, d = x2.shape
    p = w.shape[1]
    return pl.pallas_call(
        _proj_body,
        grid=(t // tm, pl.cdiv(p, tn)),
        in_specs=[pl.BlockSpec((tm, d), lambda i, j: (i, 0)),
                  pl.BlockSpec((1, d), lambda i, j: (0, 0)),
                  pl.BlockSpec((d, tn), lambda i, j: (0, j))],
        out_specs=pl.BlockSpec((tm, tn), lambda i, j: (i, j)),
        out_shape=jax.ShapeDtypeStruct((t, p), F32),
        scratch_shapes=[pltpu.VMEM((tm, d), BF16)],
        compiler_params=_cparams(("parallel", "arbitrary")),
        name="proj_in",
    )(x2, g.reshape(1, d), w)


def _out_body(ya_ref, yb_ref, yc_ref, w_ref, g_ref, x_ref, o_ref):
    bw = ya_ref.shape[1]
    m = (_dot(ya_ref[...], w_ref[0:bw, :]) + _dot(yb_ref[...], w_ref[bw:2 * bw, :])
         + _dot(yc_ref[...], w_ref[2 * bw:3 * bw, :]))
    ms = jnp.mean(m * m, axis=-1, keepdims=True)
    o_ref[...] = x_ref[...] + m * lax.rsqrt(ms + NORM_EPS) * g_ref[...]


def _out(ya, yb, yc, w, g, x2, tm=256):
    t, d = x2.shape
    bw = ya.shape[1]
    row = lambda i: (i, 0)
    fixed = lambda i: (0, 0)
    return pl.pallas_call(
        _out_body,
        grid=(t // tm,),
        in_specs=[pl.BlockSpec((tm, bw), row), pl.BlockSpec((tm, bw), row), pl.BlockSpec((tm, bw), row),
                  pl.BlockSpec((3 * bw, d), fixed), pl.BlockSpec((1, d), fixed), pl.BlockSpec((tm, d), row)],
        out_specs=pl.BlockSpec((tm, d), row),
        out_shape=jax.ShapeDtypeStruct((t, d), F32),
        compiler_params=_cparams(("parallel",)),
        name="proj_out",
    )(ya, yb, yc, w, g.reshape(1, d), x2)


def _t5_bucket(rel):
    half = NUM_BUCKETS // 2
    max_exact = half // 2
    n = jnp.abs(rel)
    nf = jnp.maximum(n, max_exact).astype(F32)
    large = max_exact + (jnp.log(nf / max_exact) / math.log(MAX_DISTANCE / max_exact)
                         * (half - max_exact)).astype(jnp.int32)
    large = jnp.minimum(large, half - 1)
    return jnp.where(rel > 0, half, 0) + jnp.where(n < max_exact, n, large)


def _bias_band(rel_bias, blk):
    i = jnp.arange(blk, dtype=jnp.int32)
    base = i[None, :] - i[:, None]
    far = jnp.full((blk, blk), 2 * blk, jnp.int32)
    rel = jnp.stack([base, base + blk, base - blk, -far, far])
    return jnp.transpose(rel_bias.astype(F32)[_t5_bucket(rel)], (3, 0, 1, 2))


def _attn_body(lam_ref, q_ref, k_ref, v_ref, g_ref, band_ref, sg_ref, o_ref, bias_s, s_s, *, lambda_init):
    qi = pl.program_id(1)
    b = pl.program_id(2)
    tq = q_ref.shape[0]
    nk = k_ref.shape[0] // tq

    @pl.when(b == 0)
    def _():
        for kj in range(nk):
            d = kj - qi
            idx = jnp.where(d == 0, 0, jnp.where(d == 1, 1, jnp.where(d == -1, 2, jnp.where(d > 0, 4, 3))))
            bias_s[kj] = band_ref[0, idx]

    lp = lam_ref[...]
    lam = (jnp.exp(jnp.sum(lp[0:1] * lp[1:2], axis=-1, keepdims=True))
           - jnp.exp(jnp.sum(lp[2:3] * lp[3:4], axis=-1, keepdims=True)) + lambda_init)

    q = q_ref[...] * (A_QK ** -0.5)
    lane = lax.broadcasted_iota(jnp.int32, q.shape, 1)
    q2 = jnp.concatenate([jnp.where(lane < A_QK, q, 0.0), jnp.where(lane >= A_QK, q, 0.0)], axis=0).astype(BF16)
    mx = None
    for kj in range(nk):
        kblk = k_ref[pl.ds(kj * tq, tq), :].astype(BF16)
        bias = bias_s[kj]
        s = _dot_nt(q2, kblk) + jnp.concatenate([bias, bias], axis=0)
        s_s[kj] = s
        mx = s if mx is None else jnp.maximum(mx, s)
    m = jnp.max(mx, axis=-1, keepdims=True)
    esum = None
    acc = None
    for kj in range(nk):
        e = jnp.exp(s_s[kj] - m)
        esum = e if esum is None else esum + e
        pv = _dot(e.astype(BF16), v_ref[pl.ds(kj * tq, tq), :].astype(BF16))
        acc = pv if acc is None else acc + pv
    out = acc / jnp.sum(esum, axis=-1, keepdims=True)
    att = out[:tq] - lam * out[tq:]
    ms = jnp.mean(att * att, axis=-1, keepdims=True)
    y = att * lax.rsqrt(ms + A_SUBLN_EPS) * sg_ref[...] * (1.0 - lambda_init)
    g = g_ref[...]
    o_ref[...] = (y * (g * _sigmoid(g))).astype(o_ref.dtype)


def _attn(proj, lam_p, band, subln_g, bsz, seq, bw, layer_idx):
    t = proj.shape[0]
    nb = bw // LANES
    heads = bw // (2 * A_QK)
    tq = ATT_BLOCK
    nq = seq // tq
    lambda_init = 0.8 - 0.6 * math.exp(-0.3 * layer_idx)
    return pl.pallas_call(
        functools.partial(_attn_body, lambda_init=lambda_init),
        grid=(heads, nq, bsz),
        in_specs=[pl.BlockSpec((4, A_QK), lambda h, qi, b: (0, 0)),
                  pl.BlockSpec((tq, LANES), lambda h, qi, b: (b * nq + qi, h)),
                  pl.BlockSpec((seq, LANES), lambda h, qi, b: (b, nb + h)),
                  pl.BlockSpec((seq, LANES), lambda h, qi, b: (b, 2 * nb + h)),
                  pl.BlockSpec((tq, LANES), lambda h, qi, b: (b * nq + qi, 3 * nb + h)),
                  pl.BlockSpec((1, 5, tq, tq), lambda h, qi, b: (h, 0, 0, 0)),
                  pl.BlockSpec((1, LANES), lambda h, qi, b: (0, 0))],
        out_specs=pl.BlockSpec((tq, LANES), lambda h, qi, b: (b * nq + qi, h)),
        out_shape=jax.ShapeDtypeStruct((t, bw), BF16),
        scratch_shapes=[pltpu.VMEM((seq // tq, tq, tq), F32), pltpu.VMEM((seq // tq, 2 * tq, tq), F32)],
        compiler_params=_cparams(("arbitrary", "arbitrary", "arbitrary")),
        name="diff_attn",
    )(lam_p, proj, proj, proj, proj, band, subln_g.reshape(1, LANES))


def _hgrn_consts(c):
    i = np.arange(c)
    lo = (i[:, None] >= i[None, :]).astype(np.float32)
    levels = int(math.log2(c))
    nmask = np.zeros((levels + 1, c, c), np.float32)
    role = np.zeros((levels + 1, c, LANES), np.float32)
    nmask[0] = np.eye(c)
    for l in range(levels):
        m = 1 << l
        nmask[l + 1] = (i[:, None] // (2 * m)) == (i[None, :] // (2 * m))
        role[l + 1] = ((i % (2 * m)) >= m)[:, None]
    return (jnp.asarray(lo, BF16), jnp.asarray(lo.T, BF16), jnp.asarray(nmask), jnp.asarray(role))


def _boundary(cum, m, rev):
    c = cum.shape[0]
    if 2 * m >= 8:
        n = c // (2 * m)
        x = cum.reshape(n, 2 * m, LANES)
        r = m if rev else m - 1
        return jnp.broadcast_to(x[:, r:r + 1, :], x.shape).reshape(c, LANES)
    x = cum.reshape(c // 8, 8, LANES)
    sub = lax.broadcasted_iota(jnp.int32, x.shape, 1)
    out = None
    for node in range(8 // (2 * m)):
        r = node * 2 * m + (m if rev else m - 1)
        bc = jnp.broadcast_to(x[:, r:r + 1, :], x.shape)
        out = bc if out is None else jnp.where(sub >= node * 2 * m, bc, out)
    return out.reshape(c, LANES)


def _hgrn_chunk(z, q, v, st, lbp, tri, nmask_ref, role_ref, rev):
    lb, log_lb, log1m_lb, lb_pos = lbp
    c = z.shape[0]
    log_sig = jnp.minimum(z, 0.0) - jnp.log1p(jnp.exp(-jnp.abs(z)))
    a2 = log1m_lb + log_sig
    lae = jnp.maximum(log_lb, a2) + jnp.log1p(jnp.exp(-jnp.abs(log_lb - a2)))
    lf = jnp.where(lb_pos, lae, log_sig)
    k = (1.0 - lb) / (1.0 + jnp.exp(z))
    cum = _split_dot(tri, lf, 1, 3)
    tot = cum[0:1] if rev else cum[c - 1:c]
    scores = _dot_nt(q.astype(BF16), k.astype(BF16)) * nmask_ref[0]
    for l in range(int(math.log2(c))):
        m = 1 << l
        e = jnp.exp(-jnp.abs(cum - _boundary(cum, m, rev)))
        second = role_ref[l + 1]
        qrole = (1.0 - second) if rev else second
        qs = (q * e * qrole).astype(BF16)
        ks = (k * e * (1.0 - qrole)).astype(BF16)
        scores = scores + _dot_nt(qs, ks) * nmask_ref[l + 1]
    o = _dot(scores.astype(BF16), v.astype(BF16)) + _dot((q * jnp.exp(cum)).astype(BF16), st.astype(BF16))
    kd_t = (k * jnp.exp(tot - cum)).T.astype(BF16)
    dcol = jnp.broadcast_to(jnp.exp(tot), (LANES, LANES)).T
    return o, dcol * st + _dot(kd_t, v.astype(BF16))


def _hgrn_body(q_ref, v_ref, zf_ref, zb_ref, g_ref, lb_ref, ng_ref, lo_ref, up_ref, nmask_ref, role_ref,
               o_ref, acc_s):
    seq = q_ref.shape[0]
    c = lo_ref.shape[0]
    n = seq // c
    acc_s[...] = jnp.zeros_like(acc_s)
    lbs = lb_ref[...]

    def lb_params(d):
        lb = lbs[d:d + 1]
        return lb, jnp.log(jnp.maximum(lb, LB_FLOOR)), jnp.log1p(-lb), lb > 0.0

    lbp_f, lbp_b = lb_params(0), lb_params(1)
    tri_lo, tri_up = lo_ref[...], up_ref[...]

    def body(i, carry):
        st_f, st_b = carry
        rf = pl.ds(pl.multiple_of(i * c, c), c)
        rb = pl.ds(pl.multiple_of((n - 1 - i) * c, c), c)
        of, st_f = _hgrn_chunk(zf_ref[rf, :], q_ref[rf, :], v_ref[rf, :], st_f, lbp_f, tri_lo,
                               nmask_ref, role_ref, False)
        acc_s[rf, :] += of
        ob, st_b = _hgrn_chunk(zb_ref[rb, :], q_ref[rb, :], v_ref[rb, :], st_b, lbp_b, tri_up,
                               nmask_ref, role_ref, True)
        acc_s[rb, :] += ob
        return st_f, st_b

    zero = jnp.zeros((LANES, LANES), F32)
    lax.fori_loop(0, n, body, (zero, zero))
    for r0 in range(0, seq, ROW_BLOCK):
        rows = pl.ds(r0, ROW_BLOCK)
        o = acc_s[rows, :]
        ms = jnp.mean(o * o, axis=-1, keepdims=True)
        g = g_ref[rows, :]
        o_ref[rows, :] = (o * lax.rsqrt(ms + NORM_EPS) * ng_ref[...] * (g * _sigmoid(g))).astype(o_ref.dtype)


def _hgrn(proj, lb_l, norm_g, bsz, seq, bw):
    t = proj.shape[0]
    nb = bw // LANES
    c0 = 8 * nb + 2
    consts = _hgrn_consts(C_CHUNK)
    col = lambda off: pl.BlockSpec((seq, LANES), lambda b, h: (b, off + h))
    fixed2 = lambda shape: pl.BlockSpec(shape, lambda b, h: (0, 0))
    fixed3 = lambda shape: pl.BlockSpec(shape, lambda b, h: (0, 0, 0))
    return pl.pallas_call(
        _hgrn_body,
        grid=(bsz, nb),
        in_specs=[col(c0), col(c0 + nb), col(c0 + 2 * nb), col(c0 + 3 * nb), col(c0 + 4 * nb),
                  pl.BlockSpec((2, LANES), lambda b, h: (0, h)), fixed2((1, LANES)),
                  fixed2(consts[0].shape), fixed2(consts[1].shape),
                  fixed3(consts[2].shape), fixed3(consts[3].shape)],
        out_specs=pl.BlockSpec((seq, LANES), lambda b, h: (b, h)),
        out_shape=jax.ShapeDtypeStruct((t, bw), BF16),
        scratch_shapes=[pltpu.VMEM((seq, LANES), F32)],
        compiler_params=_cparams(("parallel", "parallel")),
        name="hgrn2",
    )(proj, proj, proj, proj, proj, lb_l, norm_g.reshape(1, LANES), *consts)


def _rwkv_consts(c):
    n = 2 * c
    i = np.arange(n)
    same = (i[:, None] // c) == (i[None, :] // c)
    j = np.arange(c)
    lo = (j[:, None] >= j[None, :]).astype(np.float32)
    tmask = np.stack([(i[:, None] % c > i[None, :] % c), (i[:, None] % c >= i[None, :] % c),
                      (i[:, None] % c < i[None, :] % c), (i[:, None] % c <= i[None, :] % c)]).astype(np.float32)
    levels = int(math.log2(c))
    lvl = np.zeros((levels, n, n), np.float32)
    for l in range(levels):
        m = 1 << l
        lvl[l] = ((i[:, None] // (2 * m)) == (i[None, :] // (2 * m))) & ((i[:, None] // m) != (i[None, :] // m))
    head = (np.arange(LANES)[:, None] // B_HEAD) == (np.arange(LANES)[None, :] // B_HEAD)
    return (jnp.asarray(lo, BF16), jnp.asarray(lo.T, BF16), jnp.asarray(tmask * same[None]), jnp.asarray(lvl),
            jnp.asarray(head.astype(np.float32), BF16), jnp.asarray(np.eye(n, dtype=np.float32)))


def _shifted(ref, mu, t0, rows, seq):
    x = ref[pl.ds(t0, rows), :]
    row = lax.broadcasted_iota(jnp.int32, x.shape, 0)
    if t0 == 0:
        prev = jnp.where(row == 0, 0.0, pltpu.roll(x, 1, 0))
    else:
        prev = ref[pl.ds(t0 - 1, rows), :]
    if t0 + rows == seq:
        nxt = jnp.where(row == rows - 1, 0.0, pltpu.roll(x, rows - 1, 0))
    else:
        nxt = ref[pl.ds(t0 + 1, rows), :]
    return x + mu[0:1] * (prev - x) + mu[1:2] * (nxt - x)


def _stack_heads(x, lane_lo):
    return jnp.concatenate([jnp.where(lane_lo, x, 0.0), jnp.where(lane_lo, 0.0, x)], axis=0)


def _rwkv_chunk(r, lw, kk, kka, v, km, h, tri, tmask_ref, lvl_ref, eye_ref, rev):
    c = r.shape[0]
    n = 2 * c
    lane_lo = lax.broadcasted_iota(jnp.int32, (c, LANES), 1) < B_HEAD
    cw = _split_dot(tri, lw, 1, 3)
    tot = cw[0:1] if rev else cw[c - 1:c]
    e_neg = jnp.exp(-cw)
    e_tot = jnp.exp(tot - cw)
    st = lambda x: _stack_heads(x, lane_lo)
    bd2 = st(kk * jnp.exp(cw - lw))
    ad2 = st(-(kka * e_neg))
    kd2 = st(km * e_neg)
    rd2 = st(r * jnp.exp(cw))
    ae2 = st(-(kka * e_tot))
    ke2 = st(km * e_tot)
    v2 = st(v).astype(BF16)
    gram = _dot_nt(jnp.concatenate([bd2, rd2], axis=0).astype(BF16),
                   jnp.concatenate([ad2, kd2], axis=0).astype(BF16))
    strict = tmask_ref[2 if rev else 0]
    incl = tmask_ref[3 if rev else 1]
    ba = gram[:n, :n] * strict
    bk = gram[:n, n:] * strict
    ra = gram[n:, :n] * incl
    rk = gram[n:, n:] * incl
    d = eye_ref[...] + ba * lvl_ref[0]
    for l in range(1, int(math.log2(c))):
        db = d.astype(BF16)
        d = d + _dot(_dot(db, (ba * lvl_ref[l]).astype(BF16)).astype(BF16), db)
    xv = _dot(jnp.concatenate([bk, rk, ke2.T], axis=0).astype(BF16), v2)
    tz = _dot(d.astype(BF16), jnp.concatenate([xv[:n], bd2], axis=1).astype(BF16))
    p = _dot(jnp.concatenate([tz[:, LANES:], rd2], axis=0).astype(BF16), h.astype(BF16))
    u2 = tz[:, :LANES] + p[:n]
    qq = _dot(jnp.concatenate([ra, ae2.T], axis=0).astype(BF16), u2.astype(BF16))
    y2 = p[n:] + qq[:n] + xv[n:2 * n]
    dcol = jnp.broadcast_to(jnp.exp(tot), (LANES, LANES)).T
    return y2[:c] + y2[c:], dcol * h + qq[n:] + xv[2 * n:]


def _rwkv_body(r_ref, k_ref, v_ref, wd_ref, ad_ref, g_ref, mur_ref, muk_ref, muv_ref, muw_ref, mua_ref,
               w0_ref, a0_ref, wup_ref, aup_ref, kk_ref, ka_ref, rk_ref, lng_ref, lnb_ref,
               lo_ref, up_ref, tmask_ref, lvl_ref, head_ref, eye_ref,
               o_ref, r_s, v_s, kk_s, lw_s, kka_s, km_s, bv_s, y_s):
    seq = r_ref.shape[0]
    c = lo_ref.shape[0]
    n = seq // c
    rb = ROW_BLOCK
    head = head_ref[...]
    lane_lo = lax.broadcasted_iota(jnp.int32, (rb, LANES), 1) < B_LORA
    for t0 in range(0, seq, rb):
        rows = pl.ds(t0, rb)
        r = _shifted(r_ref, mur_ref[...], t0, rb, seq)
        k = _shifted(k_ref, muk_ref[...], t0, rb, seq)
        v = _shifted(v_ref, muv_ref[...], t0, rb, seq)
        twd = jnp.tanh(_shifted(wd_ref, muw_ref[...], t0, rb, seq))
        ad = _shifted(ad_ref, mua_ref[...], t0, rb, seq)
        kk = k * kk_ref[...]
        kk = kk / jnp.maximum(jnp.sqrt(_split_dot(kk * kk, head, 0, 2)), 1e-12)
        kmsum = None
        for d in range(2):
            sel = lane_lo if d == 0 else jnp.logical_not(lane_lo)
            wl = _dot(jnp.where(sel, twd, 0.0).astype(BF16), wup_ref[...].astype(BF16))
            al = _dot(jnp.where(sel, ad, 0.0).astype(BF16), aup_ref[...].astype(BF16))
            lw_s[d, rows, :] = -math.exp(-0.5) * _sigmoid(w0_ref[d:d + 1, :] + wl)
            a = _sigmoid(a0_ref[d:d + 1, :] + al)
            km = k * (1.0 + (a - 1.0) * ka_ref[...])
            kka_s[d, rows, :] = kk * a
            km_s[d, rows, :] = km
            kmsum = km if kmsum is None else kmsum + km
        bonus = _split_dot(r * kmsum * rk_ref[...], head, 0, 2)
        r_s[rows, :] = r
        v_s[rows, :] = v
        kk_s[rows, :] = kk
        bv_s[rows, :] = bonus * v
    y_s[...] = jnp.zeros_like(y_s)
    tri_lo, tri_up = lo_ref[...], up_ref[...]

    def body(i, carry):
        h_f, h_b = carry
        rf = pl.ds(pl.multiple_of(i * c, c), c)
        rv = pl.ds(pl.multiple_of((n - 1 - i) * c, c), c)
        yf, h_f = _rwkv_chunk(r_s[rf, :], lw_s[0, rf, :], kk_s[rf, :], kka_s[0, rf, :], v_s[rf, :], km_s[0, rf, :],
                              h_f, tri_lo, tmask_ref, lvl_ref, eye_ref, False)
        y_s[rf, :] += yf
        yb, h_b = _rwkv_chunk(r_s[rv, :], lw_s[1, rv, :], kk_s[rv, :], kka_s[1, rv, :], v_s[rv, :], km_s[1, rv, :],
                              h_b, tri_up, tmask_ref, lvl_ref, eye_ref, True)
        y_s[rv, :] += yb
        return h_f, h_b

    zero = jnp.zeros((LANES, LANES), F32)
    lax.fori_loop(0, n, body, (zero, zero))
    inv_n = 1.0 / B_HEAD
    for t0 in range(0, seq, rb):
        rows = pl.ds(t0, rb)
        y = y_s[rows, :]
        mu = _split_dot(y, head, 0, 2) * inv_n
        yc = y - mu
        var = _split_dot(yc * yc, head, 0, 2) * inv_n
        yn = yc * lax.rsqrt(var + B_LNX_EPS) * lng_ref[...] + lnb_ref[...]
        g = g_ref[rows, :]
        o_ref[rows, :] = ((yn + bv_s[rows, :]) * (g * _sigmoid(g))).astype(o_ref.dtype)


def _rwkv(proj, mu, w0, a0, wup, aup, k_k, k_a, r_k, lnx_g, lnx_b, bsz, seq, bw):
    t = proj.shape[0]
    nb = bw // LANES
    c0 = 4 * nb
    consts = _rwkv_consts(B_CHUNK)
    col = lambda off: pl.BlockSpec((seq, LANES), lambda b, g: (b, off + g))
    colf = lambda off: pl.BlockSpec((seq, LANES), lambda b, g: (b, off))
    par2 = lambda off: pl.BlockSpec((2, LANES), lambda b, g: (0, off + g))
    par2f = lambda off: pl.BlockSpec((2, LANES), lambda b, g: (0, off))
    par1 = pl.BlockSpec((1, LANES), lambda b, g: (0, g))
    up = pl.BlockSpec((2 * B_LORA, LANES), lambda b, g: (0, g))

    def fixed(a):
        return pl.BlockSpec(a.shape, lambda b, g: (0,) * a.ndim)

    row = lambda a: a.reshape(1, bw)
    return pl.pallas_call(
        _rwkv_body,
        grid=(bsz, nb),
        in_specs=[col(c0), col(c0 + nb), col(c0 + 2 * nb), colf(c0 + 3 * nb), colf(c0 + 3 * nb + 1), col(c0 + 3 * nb + 2),
                  par2(0), par2(nb), par2(2 * nb), par2f(3 * nb), par2f(3 * nb + 1),
                  par2(0), par2(0), up, up, par1, par1, par1, par1, par1] + [fixed(a) for a in consts],
        out_specs=pl.BlockSpec((seq, LANES), lambda b, g: (b, g)),
        out_shape=jax.ShapeDtypeStruct((t, bw), BF16),
        scratch_shapes=[pltpu.VMEM((seq, LANES), F32), pltpu.VMEM((seq, LANES), F32), pltpu.VMEM((seq, LANES), F32),
                        pltpu.VMEM((2, seq, LANES), F32), pltpu.VMEM((2, seq, LANES), F32),
                        pltpu.VMEM((2, seq, LANES), F32), pltpu.VMEM((seq, LANES), F32),
                        pltpu.VMEM((seq, LANES), F32)],
        compiler_params=_cparams(("parallel", "parallel")),
        name="rwkv7",
    )(proj, proj, proj, proj, proj, proj, mu, mu, mu, mu, mu, w0, a0,
      wup.reshape(2 * B_LORA, bw), aup.reshape(2 * B_LORA, bw),
      row(k_k), row(k_a), row(r_k), row(lnx_g), row(lnx_b), *consts)


def kernel(x, rel_bias, pre_norm_g, post_norm_g, w_in, w_out, lambda_q1, lambda_k1, lambda_q2, lambda_k2, subln_g, rwkv_shift_mu, rwkv_w0, rwkv_w_up, rwkv_a0, rwkv_a_up, rwkv_k_k, rwkv_k_a, rwkv_r_k, rwkv_lnx_g, rwkv_lnx_b, hgrn_lb_logits, hgrn_norm_g):
    bsz, seq, d = x.shape
    depth = w_in.shape[0]
    bw = d // 2
    assert bw % (2 * LANES) == 0 and seq % (2 * ATT_BLOCK) == 0 and seq % (2 * C_CHUNK) == 0
    assert w_in.shape[2] == 13 * bw + 4 * B_LORA
    x2 = x.reshape(bsz * seq, d)
    lb = jax.nn.softmax(hgrn_lb_logits.astype(F32), axis=1)
    lb = jnp.cumsum(lb, axis=1) - lb[:, :1]
    band = _bias_band(rel_bias, ATT_BLOCK)
    for l in range(depth):
        proj = _proj(x2, pre_norm_g[l], w_in[l].astype(BF16))
        lam_p = jnp.stack([lambda_q1[l], lambda_k1[l], lambda_q2[l], lambda_k2[l]]).astype(F32)
        ya = _attn(proj, lam_p, band, subln_g[l], bsz, seq, bw, l)
        yb = _rwkv(proj, rwkv_shift_mu[l], rwkv_w0[l], rwkv_a0[l], rwkv_w_up[l], rwkv_a_up[l], rwkv_k_k[l],
                   rwkv_k_a[l], rwkv_r_k[l], rwkv_lnx_g[l], rwkv_lnx_b[l], bsz, seq, bw)
        yc = _hgrn(proj, lb[:, l], hgrn_norm_g[l], bsz, seq, bw)
        x2 = _out(ya, yb, yc, w_out[l].astype(BF16), post_norm_g[l], x2)
    return x2.reshape(bsz, seq, d)
```

```python
import functools
import math

import numpy as np
import jax
import jax.numpy as jnp
from jax import lax
from jax.experimental import pallas as pl
from jax.experimental.pallas import tpu as pltpu

F32 = jnp.float32
BF16 = jnp.bfloat16

LANES = 128
VMEM_LIMIT = 52 * 1024 * 1024

NORM_EPS = 1e-6
A_QK = 64
A_SUBLN_EPS = 1e-5
NUM_BUCKETS = 32
MAX_DISTANCE = 128
ATT_BLOCK = 256
B_HEAD = 64
B_LORA = 64
B_LNX_EPS = 64e-5
B_CHUNK = 64
B_GROUP = 4
C_CHUNK = 128
LB_FLOOR = 1e-30
ROW_BLOCK = 256

_NT = (((1,), (1,)), ((), ()))


def _dot(a, b):
    return jnp.dot(a, b, preferred_element_type=F32)


def _dot_nt(a, b):
    return lax.dot_general(a, b, _NT, preferred_element_type=F32)


def _split_dot(a, b, split, passes):
    rem = a if split == 0 else b
    acc = None
    for _ in range(passes):
        piece = rem.astype(BF16)
        d = _dot(piece, b) if split == 0 else _dot(a, piece)
        acc = d if acc is None else acc + d
        rem = rem - piece.astype(F32)
    return acc


def _sigmoid(x):
    return 1.0 / (1.0 + jnp.exp(-x))


def _cparams(sem):
    return pltpu.CompilerParams(dimension_semantics=sem, vmem_limit_bytes=VMEM_LIMIT)


def _proj_body(x_ref, g_ref, w_ref, o_ref, u_s):
    @pl.when(pl.program_id(1) == 0)
    def _():
        xf = x_ref[...]
        ms = jnp.mean(xf * xf, axis=-1, keepdims=True)
        u_s[...] = (xf * lax.rsqrt(ms + NORM_EPS) * g_ref[...]).astype(BF16)

    o_ref[...] = _dot(u_s[...], w_ref[...])


def _proj(x2, g, w, tm=512, tn=1024):
    t, d = x2.shape
    p = w.shape[1]
    return pl.pallas_call(
        _proj_body,
        grid=(t // tm, pl.cdiv(p, tn)),
        in_specs=[pl.BlockSpec((tm, d), lambda i, j: (i, 0)),
                  pl.BlockSpec((1, d), lambda i, j: (0, 0)),
                  pl.BlockSpec((d, tn), lambda i, j: (0, j))],
        out_specs=pl.BlockSpec((tm, tn), lambda i, j: (i, j)),
        out_shape=jax.ShapeDtypeStruct((t, p), F32),
        scratch_shapes=[pltpu.VMEM((tm, d), BF16)],
        compiler_params=_cparams(("parallel", "arbitrary")),
        name="proj_in",
    )(x2, g.reshape(1, d), w)


def _out_body(ya_ref, yb_ref, yc_ref, w_ref, g_ref, x_ref, o_ref):
    bw = ya_ref.shape[1]
    m = (_dot(ya_ref[...], w_ref[0:bw, :]) + _dot(yb_ref[...], w_ref[bw:2 * bw, :])
         + _dot(yc_ref[...], w_ref[2 * bw:3 * bw, :]))
    ms = jnp.mean(m * m, axis=-1, keepdims=True)
    o_ref[...] = x_ref[...] + m * lax.rsqrt(ms + NORM_EPS) * g_ref[...]


def _out(ya, yb, yc, w, g, x2, tm=256):
    t, d = x2.shape
    bw = ya.shape[1]
    row = lambda i: (i, 0)
    fixed = lambda i: (0, 0)
    return pl.pallas_call(
        _out_body,
        grid=(t // tm,),
        in_specs=[pl.BlockSpec((tm, bw), row), pl.BlockSpec((tm, bw), row), pl.BlockSpec((tm, bw), row),
                  pl.BlockSpec((3 * bw, d), fixed), pl.BlockSpec((1, d), fixed), pl.BlockSpec((tm, d), row)],
        out_specs=pl.BlockSpec((tm, d), row),
        out_shape=jax.ShapeDtypeStruct((t, d), F32),
        compiler_params=_cparams(("parallel",)),
        name="proj_out",
    )(ya, yb, yc, w, g.reshape(1, d), x2)


def _t5_bucket(rel):
    half = NUM_BUCKETS // 2
    max_exact = half // 2
    n = jnp.abs(rel)
    nf = jnp.maximum(n, max_exact).astype(F32)
    large = max_exact + (jnp.log(nf / max_exact) / math.log(MAX_DISTANCE / max_exact)
                         * (half - max_exact)).astype(jnp.int32)
    large = jnp.minimum(large, half - 1)
    return jnp.where(rel > 0, half, 0) + jnp.where(n < max_exact, n, large)


def _bias_band(rel_bias, blk):
    c = jnp.arange(2 * blk, dtype=jnp.int32) - blk
    far = jnp.full((2 * blk,), 2 * blk, jnp.int32)
    rel = jnp.stack([c, c + blk, c - blk, -far, far, far, far, far])
    return jnp.transpose(rel_bias.astype(F32)[_t5_bucket(rel)], (2, 0, 1))


def _attn_body(lam_ref, q_ref, k_ref, v_ref, g_ref, band_ref, sg_ref, o_ref, tile_s, bias_s, s_s, *, lambda_init):
    qi = pl.program_id(1)
    b = pl.program_id(2)
    tq = q_ref.shape[0]
    nk = k_ref.shape[0] // tq

    @pl.when((b == 0) & (qi == 0))
    def _():
        vec = band_ref[0]
        for d in range(3):
            w = jnp.broadcast_to(vec[d:d + 1, :], (tq, 2 * tq))
            tile_s[d] = pltpu.roll(w, tq, 1, stride=1, stride_axis=0)[:, :tq]
        for d in range(3, 5):
            tile_s[d] = jnp.broadcast_to(vec[d:d + 1, :tq], (tq, tq))

    @pl.when(b == 0)
    def _():
        for kj in range(nk):
            d = kj - qi
            idx = jnp.where(d == 0, 0, jnp.where(d == 1, 1, jnp.where(d == -1, 2, jnp.where(d > 0, 4, 3))))
            bias_s[kj] = tile_s[idx]

    lp = lam_ref[...]
    lam = (jnp.exp(jnp.sum(lp[0:1] * lp[1:2], axis=-1, keepdims=True))
           - jnp.exp(jnp.sum(lp[2:3] * lp[3:4], axis=-1, keepdims=True)) + lambda_init)

    q = q_ref[...] * (A_QK ** -0.5)
    lane = lax.broadcasted_iota(jnp.int32, q.shape, 1)
    q2 = jnp.concatenate([jnp.where(lane < A_QK, q, 0.0), jnp.where(lane >= A_QK, q, 0.0)], axis=0).astype(BF16)
    mx = None
    for kj in range(nk):
        kblk = k_ref[pl.ds(kj * tq, tq), :].astype(BF16)
        bias = bias_s[kj]
        s = _dot_nt(q2, kblk) + jnp.concatenate([bias, bias], axis=0)
        s_s[kj] = s
        mx = s if mx is None else jnp.maximum(mx, s)
    m = jnp.max(mx, axis=-1, keepdims=True)
    esum = None
    acc = None
    for kj in range(nk):
        e = jnp.exp(s_s[kj] - m)
        esum = e if esum is None else esum + e
        pv = _dot(e.astype(BF16), v_ref[pl.ds(kj * tq, tq), :].astype(BF16))
        acc = pv if acc is None else acc + pv
    out = acc / jnp.sum(esum, axis=-1, keepdims=True)
    att = out[:tq] - lam * out[tq:]
    ms = jnp.mean(att * att, axis=-1, keepdims=True)
    y = att * lax.rsqrt(ms + A_SUBLN_EPS) * sg_ref[...] * (1.0 - lambda_init)
    g = g_ref[...]
    o_ref[...] = (y * (g * _sigmoid(g))).astype(o_ref.dtype)


def _attn(proj, lam_p, band, subln_g, bsz, seq, bw, layer_idx):
    t = proj.shape[0]
    nb = bw // LANES
    heads = bw // (2 * A_QK)
    tq = ATT_BLOCK
    nq = seq // tq
    lambda_init = 0.8 - 0.6 * math.exp(-0.3 * layer_idx)
    return pl.pallas_call(
        functools.partial(_attn_body, lambda_init=lambda_init),
        grid=(heads, nq, bsz),
        in_specs=[pl.BlockSpec((4, A_QK), lambda h, qi, b: (0, 0)),
                  pl.BlockSpec((tq, LANES), lambda h, qi, b: (b * nq + qi, h)),
                  pl.BlockSpec((seq, LANES), lambda h, qi, b: (b, nb + h)),
                  pl.BlockSpec((seq, LANES), lambda h, qi, b: (b, 2 * nb + h)),
                  pl.BlockSpec((tq, LANES), lambda h, qi, b: (b * nq + qi, 3 * nb + h)),
                  pl.BlockSpec((1, 8, 2 * tq), lambda h, qi, b: (h, 0, 0)),
                  pl.BlockSpec((1, LANES), lambda h, qi, b: (0, 0))],
        out_specs=pl.BlockSpec((tq, LANES), lambda h, qi, b: (b * nq + qi, h)),
        out_shape=jax.ShapeDtypeStruct((t, bw), BF16),
        scratch_shapes=[pltpu.VMEM((5, tq, tq), F32), pltpu.VMEM((seq // tq, tq, tq), F32),
                        pltpu.VMEM((seq // tq, 2 * tq, tq), F32)],
        compiler_params=_cparams(("arbitrary", "arbitrary", "arbitrary")),
        name="diff_attn",
    )(lam_p, proj, proj, proj, proj, band, subln_g.reshape(1, LANES))


def _hgrn_consts(c):
    i = np.arange(c)
    lo = (i[:, None] >= i[None, :]).astype(np.float32)
    levels = int(math.log2(c))
    nmask = np.zeros((levels + 1, c, c), np.float32)
    role = np.zeros((levels + 1, c, LANES), np.float32)
    nmask[0] = np.eye(c)
    for l in range(levels):
        m = 1 << l
        nmask[l + 1] = (i[:, None] // (2 * m)) == (i[None, :] // (2 * m))
        role[l + 1] = ((i % (2 * m)) >= m)[:, None]
    return (jnp.asarray(lo, BF16), jnp.asarray(lo.T, BF16), jnp.asarray(nmask), jnp.asarray(role))


def _boundary(cum, m, rev):
    c = cum.shape[0]
    if 2 * m >= 8:
        n = c // (2 * m)
        x = cum.reshape(n, 2 * m, LANES)
        r = m if rev else m - 1
        return jnp.broadcast_to(x[:, r:r + 1, :], x.shape).reshape(c, LANES)
    x = cum.reshape(c // 8, 8, LANES)
    sub = lax.broadcasted_iota(jnp.int32, x.shape, 1)
    out = None
    for node in range(8 // (2 * m)):
        r = node * 2 * m + (m if rev else m - 1)
        bc = jnp.broadcast_to(x[:, r:r + 1, :], x.shape)
        out = bc if out is None else jnp.where(sub >= node * 2 * m, bc, out)
    return out.reshape(c, LANES)


def _hgrn_chunk(z, q, v, st, lbp, tri, nmask_ref, role_ref, rev):
    lb, log_lb, log1m_lb, lb_pos = lbp
    c = z.shape[0]
    log_sig = jnp.minimum(z, 0.0) - jnp.log1p(jnp.exp(-jnp.abs(z)))
    a2 = log1m_lb + log_sig
    lae = jnp.maximum(log_lb, a2) + jnp.log1p(jnp.exp(-jnp.abs(log_lb - a2)))
    lf = jnp.where(lb_pos, lae, log_sig)
    k = (1.0 - lb) / (1.0 + jnp.exp(z))
    cum = _split_dot(tri, lf, 1, 3)
    tot = cum[0:1] if rev else cum[c - 1:c]
    scores = _dot_nt(q.astype(BF16), k.astype(BF16)) * nmask_ref[0]
    for l in range(int(math.log2(c))):
        m = 1 << l
        e = jnp.exp(-jnp.abs(cum - _boundary(cum, m, rev)))
        second = role_ref[l + 1]
        qrole = (1.0 - second) if rev else second
        qs = (q * e * qrole).astype(BF16)
        ks = (k * e * (1.0 - qrole)).astype(BF16)
        scores = scores + _dot_nt(qs, ks) * nmask_ref[l + 1]
    o = _dot(scores.astype(BF16), v.astype(BF16)) + _dot((q * jnp.exp(cum)).astype(BF16), st.astype(BF16))
    kd_t = (k * jnp.exp(tot - cum)).T.astype(BF16)
    dcol = jnp.broadcast_to(jnp.exp(tot), (LANES, LANES)).T
    return o, dcol * st + _dot(kd_t, v.astype(BF16))


def _hgrn_body(q_ref, v_ref, zf_ref, zb_ref, g_ref, lb_ref, ng_ref, lo_ref, up_ref, nmask_ref, role_ref,
               o_ref, acc_s):
    seq = q_ref.shape[0]
    c = lo_ref.shape[0]
    n = seq // c
    acc_s[...] = jnp.zeros_like(acc_s)
    lbs = lb_ref[...]

    def lb_params(d):
        lb = lbs[d:d + 1]
        return lb, jnp.log(jnp.maximum(lb, LB_FLOOR)), jnp.log1p(-lb), lb > 0.0

    lbp_f, lbp_b = lb_params(0), lb_params(1)
    tri_lo, tri_up = lo_ref[...], up_ref[...]

    def body(i, carry):
        st_f, st_b = carry
        rf = pl.ds(pl.multiple_of(i * c, c), c)
        rb = pl.ds(pl.multiple_of((n - 1 - i) * c, c), c)
        of, st_f = _hgrn_chunk(zf_ref[rf, :], q_ref[rf, :], v_ref[rf, :], st_f, lbp_f, tri_lo,
                               nmask_ref, role_ref, False)
        acc_s[rf, :] += of
        ob, st_b = _hgrn_chunk(zb_ref[rb, :], q_ref[rb, :], v_ref[rb, :], st_b, lbp_b, tri_up,
                               nmask_ref, role_ref, True)
        acc_s[rb, :] += ob
        return st_f, st_b

    zero = jnp.zeros((LANES, LANES), F32)
    lax.fori_loop(0, n, body, (zero, zero))
    for r0 in range(0, seq, ROW_BLOCK):
        rows = pl.ds(r0, ROW_BLOCK)
        o = acc_s[rows, :]
        ms = jnp.mean(o * o, axis=-1, keepdims=True)
        g = g_ref[rows, :]
        o_ref[rows, :] = (o * lax.rsqrt(ms + NORM_EPS) * ng_ref[...] * (g * _sigmoid(g))).astype(o_ref.dtype)


def _hgrn(proj, lb_l, norm_g, bsz, seq, bw):
    t = proj.shape[0]
    nb = bw // LANES
    c0 = 8 * nb + 2
    consts = _hgrn_consts(C_CHUNK)
    col = lambda off: pl.BlockSpec((seq, LANES), lambda b, h: (b, off + h))
    fixed2 = lambda shape: pl.BlockSpec(shape, lambda b, h: (0, 0))
    fixed3 = lambda shape: pl.BlockSpec(shape, lambda b, h: (0, 0, 0))
    return pl.pallas_call(
        _hgrn_body,
        grid=(bsz, nb),
        in_specs=[col(c0), col(c0 + nb), col(c0 + 2 * nb), col(c0 + 3 * nb), col(c0 + 4 * nb),
                  pl.BlockSpec((2, LANES), lambda b, h: (0, h)), fixed2((1, LANES)),
                  fixed2(consts[0].shape), fixed2(consts[1].shape),
                  fixed3(consts[2].shape), fixed3(consts[3].shape)],
        out_specs=pl.BlockSpec((seq, LANES), lambda b, h: (b, h)),
        out_shape=jax.ShapeDtypeStruct((t, bw), BF16),
        scratch_shapes=[pltpu.VMEM((seq, LANES), F32)],
        compiler_params=_cparams(("parallel", "parallel")),
        name="hgrn2",
    )(proj, proj, proj, proj, proj, lb_l, norm_g.reshape(1, LANES), *consts)


def _rwkv_consts(c, group):
    n = 2 * c
    i = np.arange(n)
    same = (i[:, None] // c) == (i[None, :] // c)
    j = np.arange(group * c)
    lo = ((j[:, None] >= j[None, :]) & (j[:, None] // c == j[None, :] // c)).astype(np.float32)
    tmask = np.stack([(i[:, None] % c > i[None, :] % c), (i[:, None] % c >= i[None, :] % c),
                      (i[:, None] % c < i[None, :] % c), (i[:, None] % c <= i[None, :] % c)]).astype(np.float32)
    levels = int(math.log2(c))
    lvl = np.zeros((levels, n, n), np.float32)
    for l in range(levels):
        m = 1 << l
        lvl[l] = ((i[:, None] // (2 * m)) == (i[None, :] // (2 * m))) & ((i[:, None] // m) != (i[None, :] // m))
    head = (np.arange(LANES)[:, None] // B_HEAD) == (np.arange(LANES)[None, :] // B_HEAD)
    return (jnp.asarray(lo, BF16), jnp.asarray(lo.T, BF16), jnp.asarray(tmask * same[None]), jnp.asarray(lvl),
            jnp.asarray(head.astype(np.float32), BF16), jnp.asarray(np.eye(n, dtype=np.float32)))


def _shifted(ref, mu, t0, rows, seq):
    x = ref[pl.ds(t0, rows), :]
    row = lax.broadcasted_iota(jnp.int32, x.shape, 0)
    if t0 == 0:
        prev = jnp.where(row == 0, 0.0, pltpu.roll(x, 1, 0))
    else:
        prev = ref[pl.ds(t0 - 1, rows), :]
    if t0 + rows == seq:
        nxt = jnp.where(row == rows - 1, 0.0, pltpu.roll(x, rows - 1, 0))
    else:
        nxt = ref[pl.ds(t0 + 1, rows), :]
    return x + mu[0:1] * (prev - x) + mu[1:2] * (nxt - x)


def _stack_heads(x, lane_lo):
    return jnp.concatenate([jnp.where(lane_lo, x, 0.0), jnp.where(lane_lo, 0.0, x)], axis=0)


def _rwkv_prep(r, lw, kk, kka, v, km, tri, tmask_ref, lvl_ref, eye_ref, rev, group):
    c = r.shape[0] // group
    n = 2 * c
    lane_lo = lax.broadcasted_iota(jnp.int32, (c, LANES), 1) < B_HEAD
    cw_all = _split_dot(tri, lw, 1, 3)
    strict = tmask_ref[2 if rev else 0]
    incl = tmask_ref[3 if rev else 1]
    eye = eye_ref[...]
    rows = [slice(i * c, (i + 1) * c) for i in range(group)]
    cw = [cw_all[s] for s in rows]
    tot = [x[0:1] if rev else x[c - 1:c] for x in cw]
    e_neg = [jnp.exp(-x) for x in cw]
    e_tot = [jnp.exp(t - x) for t, x in zip(tot, cw)]
    st = lambda x: _stack_heads(x, lane_lo)
    bd2 = [st(kk[s] * jnp.exp(x - lw[s])) for s, x in zip(rows, cw)]
    ad2 = [st(-(kka[s] * e)) for s, e in zip(rows, e_neg)]
    kd2 = [st(km[s] * e) for s, e in zip(rows, e_neg)]
    rd2 = [st(r[s] * jnp.exp(x)) for s, x in zip(rows, cw)]
    ae2_t = [st(-(kka[s] * e)).T for s, e in zip(rows, e_tot)]
    ke2_t = [st(km[s] * e).T for s, e in zip(rows, e_tot)]
    v2 = [st(v[s]).astype(BF16) for s in rows]
    gram = [_dot_nt(jnp.concatenate([b, q], axis=0).astype(BF16), jnp.concatenate([a, k], axis=0).astype(BF16))
            for b, q, a, k in zip(bd2, rd2, ad2, kd2)]
    ba = [g[:n, :n] * strict for g in gram]
    bk = [g[:n, n:] * strict for g in gram]
    ra = [g[n:, :n] * incl for g in gram]
    rk = [g[n:, n:] * incl for g in gram]
    d = [eye + x * lvl_ref[0] for x in ba]
    for l in range(1, int(math.log2(c))):
        lv = lvl_ref[l]
        db = [x.astype(BF16) for x in d]
        dm = [_dot(x, (a * lv).astype(BF16)).astype(BF16) for x, a in zip(db, ba)]
        d = [x + _dot(y, z) for x, y, z in zip(d, dm, db)]
    xv = [_dot(jnp.concatenate([a, b, k], axis=0).astype(BF16), vv) for a, b, k, vv in zip(bk, rk, ke2_t, v2)]
    tz = [_dot(x.astype(BF16), jnp.concatenate([y[:n], b], axis=1).astype(BF16)) for x, y, b in zip(d, xv, bd2)]
    tr = [jnp.concatenate([t[:, LANES:], q], axis=0).astype(BF16) for t, q in zip(tz, rd2)]
    rae = [jnp.concatenate([a, e], axis=0).astype(BF16) for a, e in zip(ra, ae2_t)]
    dcol = [jnp.broadcast_to(jnp.exp(t), (LANES, LANES)).T for t in tot]
    add = [jnp.concatenate([t[:, :LANES], y[n:], dc], axis=0) for t, y, dc in zip(tz, xv, dcol)]
    return tr, rae, add


def _rwkv_step(tr, rae, add, h):
    n = tr.shape[0] // 2
    p = _dot(tr, h.astype(BF16))
    u2 = add[0:n] + p[:n]
    qq = _dot(rae, u2.astype(BF16))
    y2 = p[n:] + qq[:n] + add[n:2 * n]
    return y2[:n // 2] + y2[n // 2:], add[3 * n:4 * n] * h + qq[n:] + add[2 * n:3 * n]


def _rwkv_body(r_ref, k_ref, v_ref, wd_ref, ad_ref, g_ref, mur_ref, muk_ref, muv_ref, muw_ref, mua_ref,
               w0_ref, a0_ref, wup_ref, aup_ref, kk_ref, ka_ref, rk_ref, lng_ref, lnb_ref,
               lo_ref, up_ref, tmask_ref, lvl_ref, head_ref, eye_ref,
               o_ref, r_s, v_s, kk_s, lw_s, kka_s, km_s, bv_s, y_s, tr_s, rae_s, add_s):
    seq = r_ref.shape[0]
    c = B_CHUNK
    n = seq // c
    rb = ROW_BLOCK
    head = head_ref[...]
    lane_lo = lax.broadcasted_iota(jnp.int32, (rb, LANES), 1) < B_LORA
    for t0 in range(0, seq, rb):
        rows = pl.ds(t0, rb)
        r = _shifted(r_ref, mur_ref[...], t0, rb, seq)
        k = _shifted(k_ref, muk_ref[...], t0, rb, seq)
        v = _shifted(v_ref, muv_ref[...], t0, rb, seq)
        twd = jnp.tanh(_shifted(wd_ref, muw_ref[...], t0, rb, seq))
        ad = _shifted(ad_ref, mua_ref[...], t0, rb, seq)
        kk = k * kk_ref[...]
        kk = kk / jnp.maximum(jnp.sqrt(_split_dot(kk * kk, head, 0, 2)), 1e-12)
        kmsum = None
        for d in range(2):
            sel = lane_lo if d == 0 else jnp.logical_not(lane_lo)
            wl = _dot(jnp.where(sel, twd, 0.0).astype(BF16), wup_ref[...].astype(BF16))
            al = _dot(jnp.where(sel, ad, 0.0).astype(BF16), aup_ref[...].astype(BF16))
            lw_s[d, rows, :] = -math.exp(-0.5) * _sigmoid(w0_ref[d:d + 1, :] + wl)
            a = _sigmoid(a0_ref[d:d + 1, :] + al)
            km = k * (1.0 + (a - 1.0) * ka_ref[...])
            kka_s[d, rows, :] = kk * a
            km_s[d, rows, :] = km
            kmsum = km if kmsum is None else kmsum + km
        bonus = _split_dot(r * kmsum * rk_ref[...], head, 0, 2)
        r_s[rows, :] = r
        v_s[rows, :] = v
        kk_s[rows, :] = kk
        bv_s[rows, :] = bonus * v
    y_s[...] = jnp.zeros_like(y_s)
    tri_lo, tri_up = lo_ref[...], up_ref[...]
    group = B_GROUP
    gc = group * c

    def body(i, carry):
        h_f, h_b = carry
        base_f = pl.multiple_of(i * gc, gc)
        base_b = pl.multiple_of((n // group - 1 - i) * gc, gc)
        for d, base, tri in ((0, base_f, tri_lo), (1, base_b, tri_up)):
            rows = pl.ds(base, gc)
            tr, rae, add = _rwkv_prep(r_s[rows, :], lw_s[d, rows, :], kk_s[rows, :], kka_s[d, rows, :], v_s[rows, :],
                                      km_s[d, rows, :], tri, tmask_ref, lvl_ref, eye_ref, d == 1, group)
            for j in range(group):
                tr_s[d, j] = tr[j]
                rae_s[d, j] = rae[j]
                add_s[d, j] = add[j]
        for j in range(group):
            jb = group - 1 - j
            yf, h_f = _rwkv_step(tr_s[0, j], rae_s[0, j], add_s[0, j], h_f)
            yb, h_b = _rwkv_step(tr_s[1, jb], rae_s[1, jb], add_s[1, jb], h_b)
            y_s[pl.ds(base_f + j * c, c), :] += yf
            y_s[pl.ds(base_b + jb * c, c), :] += yb
        return h_f, h_b

    zero = jnp.zeros((LANES, LANES), F32)
    lax.fori_loop(0, n // group, body, (zero, zero))
    inv_n = 1.0 / B_HEAD
    for t0 in range(0, seq, rb):
        rows = pl.ds(t0, rb)
        y = y_s[rows, :]
        mu = _split_dot(y, head, 0, 2) * inv_n
        yc = y - mu
        var = _split_dot(yc * yc, head, 0, 2) * inv_n
        yn = yc * lax.rsqrt(var + B_LNX_EPS) * lng_ref[...] + lnb_ref[...]
        g = g_ref[rows, :]
        o_ref[rows, :] = ((yn + bv_s[rows, :]) * (g * _sigmoid(g))).astype(o_ref.dtype)


def _rwkv(proj, mu, w0, a0, wup, aup, k_k, k_a, r_k, lnx_g, lnx_b, bsz, seq, bw):
    t = proj.shape[0]
    nb = bw // LANES
    c0 = 4 * nb
    consts = _rwkv_consts(B_CHUNK, B_GROUP)
    col = lambda off: pl.BlockSpec((seq, LANES), lambda b, g: (b, off + g))
    colf = lambda off: pl.BlockSpec((seq, LANES), lambda b, g: (b, off))
    par2 = lambda off: pl.BlockSpec((2, LANES), lambda b, g: (0, off + g))
    par2f = lambda off: pl.BlockSpec((2, LANES), lambda b, g: (0, off))
    par1 = pl.BlockSpec((1, LANES), lambda b, g: (0, g))
    up = pl.BlockSpec((2 * B_LORA, LANES), lambda b, g: (0, g))

    def fixed(a):
        return pl.BlockSpec(a.shape, lambda b, g: (0,) * a.ndim)

    row = lambda a: a.reshape(1, bw)
    return pl.pallas_call(
        _rwkv_body,
        grid=(bsz, nb),
        in_specs=[col(c0), col(c0 + nb), col(c0 + 2 * nb), colf(c0 + 3 * nb), colf(c0 + 3 * nb + 1), col(c0 + 3 * nb + 2),
                  par2(0), par2(nb), par2(2 * nb), par2f(3 * nb), par2f(3 * nb + 1),
                  par2(0), par2(0), up, up, par1, par1, par1, par1, par1] + [fixed(a) for a in consts],
        out_specs=pl.BlockSpec((seq, LANES), lambda b, g: (b, g)),
        out_shape=jax.ShapeDtypeStruct((t, bw), BF16),
        scratch_shapes=[pltpu.VMEM((seq, LANES), F32), pltpu.VMEM((seq, LANES), F32), pltpu.VMEM((seq, LANES), F32),
                        pltpu.VMEM((2, seq, LANES), F32), pltpu.VMEM((2, seq, LANES), F32),
                        pltpu.VMEM((2, seq, LANES), F32), pltpu.VMEM((seq, LANES), F32),
                        pltpu.VMEM((seq, LANES), F32),
                        pltpu.VMEM((2, B_GROUP, 4 * B_CHUNK, LANES), BF16),
                        pltpu.VMEM((2, B_GROUP, 4 * B_CHUNK, LANES), BF16),
                        pltpu.VMEM((2, B_GROUP, 8 * B_CHUNK, LANES), F32)],
        compiler_params=_cparams(("parallel", "parallel")),
        name="rwkv7",
    )(proj, proj, proj, proj, proj, proj, mu, mu, mu, mu, mu, w0, a0,
      wup.reshape(2 * B_LORA, bw), aup.reshape(2 * B_LORA, bw),
      row(k_k), row(k_a), row(r_k), row(lnx_g), row(lnx_b), *consts)


def kernel(x, rel_bias, pre_norm_g, post_norm_g, w_in, w_out, lambda_q1, lambda_k1, lambda_q2, lambda_k2, subln_g, rwkv_shift_mu, rwkv_w0, rwkv_w_up, rwkv_a0, rwkv_a_up, rwkv_k_k, rwkv_k_a, rwkv_r_k, rwkv_lnx_g, rwkv_lnx_b, hgrn_lb_logits, hgrn_norm_g):
    bsz, seq, d = x.shape
    depth = w_in.shape[0]
    bw = d // 2
    assert bw % (2 * LANES) == 0 and seq % (2 * ATT_BLOCK) == 0 and seq % (2 * C_CHUNK) == 0
    assert seq % (B_GROUP * B_CHUNK) == 0
    assert w_in.shape[2] == 13 * bw + 4 * B_LORA
    x2 = x.reshape(bsz * seq, d)
    lb = jax.nn.softmax(hgrn_lb_logits.astype(F32), axis=1)
    lb = jnp.cumsum(lb, axis=1) - lb[:, :1]
    band = _bias_band(rel_bias, ATT_BLOCK)
    for l in range(depth):
        proj = _proj(x2, pre_norm_g[l], w_in[l].astype(BF16))
        lam_p = jnp.stack([lambda_q1[l], lambda_k1[l], lambda_q2[l], lambda_k2[l]]).astype(F32)
        ya = _attn(proj, lam_p, band, subln_g[l], bsz, seq, bw, l)
        yb = _rwkv(proj, rwkv_shift_mu[l], rwkv_w0[l], rwkv_a0[l], rwkv_w_up[l], rwkv_a_up[l], rwkv_k_k[l],
                   rwkv_k_a[l], rwkv_r_k[l], rwkv_lnx_g[l], rwkv_lnx_b[l], bsz, seq, bw)
        yc = _hgrn(proj, lb[:, l], hgrn_norm_g[l], bsz, seq, bw)
        x2 = _out(ya, yb, yc, w_out[l].astype(BF16), post_norm_g[l], x2)
    return x2.reshape(bsz, seq, d)
```

```python
import functools
import math

import numpy as np
import jax
import jax.numpy as jnp
from jax import lax
from jax.experimental import pallas as pl
from jax.experimental.pallas import tpu as pltpu

F32 = jnp.float32
BF16 = jnp.bfloat16

LANES = 128
BF16_ROWS = 16
VMEM_LIMIT = 52 * 1024 * 1024

LOG2E = math.log2(math.e)

NORM_EPS = 1e-6
A_QK = 64
A_SUBLN_EPS = 1e-5
NUM_BUCKETS = 32
MAX_DISTANCE = 128
ATT_BLOCK = 512
B_HEAD = 64
B_LORA = 64
B_LNX_EPS = 64e-5
B_CHUNK = 64
B_GROUP = 8
C_CHUNK = 128
C_GROUP = 4
LB_FLOOR = 1e-30
ROW_BLOCK = 256

_NT = (((1,), (1,)), ((), ()))
_TN = (((0,), (0,)), ((), ()))


def _dot(a, b):
    return jnp.dot(a, b, preferred_element_type=F32)


def _dot_nt(a, b):
    return lax.dot_general(a, b, _NT, preferred_element_type=F32)


def _split_dot(a, b, split, passes):
    rem = a if split == 0 else b
    acc = None
    for _ in range(passes):
        piece = rem.astype(BF16)
        d = _dot(piece, b) if split == 0 else _dot(a, piece)
        acc = d if acc is None else acc + d
        rem = rem - piece.astype(F32)
    return acc


def _sigmoid(x):
    return 1.0 / (1.0 + jnp.exp(-x))


def _cparams(sem):
    return pltpu.CompilerParams(dimension_semantics=sem, vmem_limit_bytes=VMEM_LIMIT)


def _proj_body(x_ref, g_ref, w_ref, o_ref, u_s):
    @pl.when(pl.program_id(1) == 0)
    def _():
        xf = x_ref[...]
        ms = jnp.mean(xf * xf, axis=-1, keepdims=True)
        u_s[...] = (xf * lax.rsqrt(ms + NORM_EPS) * g_ref[...]).astype(BF16)

    o_ref[...] = _dot(u_s[...], w_ref[...])


def _proj(x2, g, w, tm=512, tn=1024):
    t, d = x2.shape
    p = w.shape[1]
    return pl.pallas_call(
        _proj_body,
        grid=(t // tm, pl.cdiv(p, tn)),
        in_specs=[pl.BlockSpec((tm, d), lambda i, j: (i, 0)),
                  pl.BlockSpec((1, d), lambda i, j: (0, 0)),
                  pl.BlockSpec((d, tn), lambda i, j: (0, j))],
        out_specs=pl.BlockSpec((tm, tn), lambda i, j: (i, j)),
        out_shape=jax.ShapeDtypeStruct((t, p), F32),
        scratch_shapes=[pltpu.VMEM((tm, d), BF16)],
        compiler_params=_cparams(("parallel", "arbitrary")),
        name="proj_in",
    )(x2, g.reshape(1, d), w)


def _out_body(ya_ref, yb_ref, yc_ref, w_ref, g_ref, x_ref, o_ref):
    bw = ya_ref.shape[1]
    m = (_dot(ya_ref[...], w_ref[0:bw, :]) + _dot(yb_ref[...], w_ref[bw:2 * bw, :])
         + _dot(yc_ref[...], w_ref[2 * bw:3 * bw, :]))
    ms = jnp.mean(m * m, axis=-1, keepdims=True)
    o_ref[...] = x_ref[...] + m * lax.rsqrt(ms + NORM_EPS) * g_ref[...]


def _out(ya, yb, yc, w, g, x2, tm=256):
    t, d = x2.shape
    bw = ya.shape[1]
    row = lambda i: (i, 0)
    fixed = lambda i: (0, 0)
    return pl.pallas_call(
        _out_body,
        grid=(t // tm,),
        in_specs=[pl.BlockSpec((tm, bw), row), pl.BlockSpec((tm, bw), row), pl.BlockSpec((tm, bw), row),
                  pl.BlockSpec((3 * bw, d), fixed), pl.BlockSpec((1, d), fixed), pl.BlockSpec((tm, d), row)],
        out_specs=pl.BlockSpec((tm, d), row),
        out_shape=jax.ShapeDtypeStruct((t, d), F32),
        compiler_params=_cparams(("parallel",)),
        name="proj_out",
    )(ya, yb, yc, w, g.reshape(1, d), x2)


def _t5_bucket(rel):
    half = NUM_BUCKETS // 2
    max_exact = half // 2
    n = jnp.abs(rel)
    nf = jnp.maximum(n, max_exact).astype(F32)
    large = max_exact + (jnp.log(nf / max_exact) / math.log(MAX_DISTANCE / max_exact)
                         * (half - max_exact)).astype(jnp.int32)
    large = jnp.minimum(large, half - 1)
    return jnp.where(rel > 0, half, 0) + jnp.where(n < max_exact, n, large)


def _bias_band(rel_bias, blk):
    c = blk - jnp.arange(2 * blk, dtype=jnp.int32)
    far = jnp.full((2 * blk,), 2 * blk, jnp.int32)
    rel = jnp.stack([c, c + blk, c - blk, -far, far, far, far, far])
    return jnp.transpose(rel_bias.astype(F32)[_t5_bucket(rel)], (2, 0, 1))


def _attn_body(lam_ref, q_ref, k_ref, v_ref, g_ref, band_ref, sg_ref, o_ref, tile_s, kb_s, vt_s, s_s, *,
               lambda_init):
    b = pl.program_id(1)
    qi = pl.program_id(2)
    tq = q_ref.shape[0]
    nk = k_ref.shape[0] // tq

    @pl.when((b == 0) & (qi == 0))
    def _():
        vec = band_ref[0] * LOG2E
        for d in range(3):
            w = jnp.broadcast_to(vec[d:d + 1, :], (tq, 2 * tq))
            tile_s[d] = pltpu.roll(w, tq, 1, stride=1, stride_axis=0)[:, :tq]
        for d in range(3, 5):
            tile_s[d] = jnp.broadcast_to(vec[d:d + 1, :tq], (tq, tq))

    @pl.when(qi == 0)
    def _():
        row = lax.broadcasted_iota(jnp.int32, (BF16_ROWS, k_ref.shape[0]), 0)
        vt_s[LANES:LANES + BF16_ROWS, :] = jnp.where(row == 0, 1.0, 0.0).astype(BF16)
        for kj in range(nk):
            rows = pl.ds(kj * tq, tq)
            kb_s[rows, :] = k_ref[rows, :].astype(BF16)
            vt_s[0:LANES, kj * tq:(kj + 1) * tq] = v_ref[rows, :].T.astype(BF16)

    lp = lam_ref[...]
    lam = (jnp.exp(jnp.sum(lp[0:1] * lp[1:2], axis=-1, keepdims=True))
           - jnp.exp(jnp.sum(lp[2:3] * lp[3:4], axis=-1, keepdims=True)) + lambda_init)

    q = q_ref[...] * (A_QK ** -0.5 * LOG2E)
    lane = lax.broadcasted_iota(jnp.int32, q.shape, 1)
    q2 = jnp.concatenate([jnp.where(lane < A_QK, q, 0.0), jnp.where(lane >= A_QK, q, 0.0)], axis=0).astype(BF16)
    mx = None
    for kj in range(nk + 1):
        if kj < nk:
            s_s[kj] = _dot_nt(kb_s[pl.ds(kj * tq, tq), :], q2)
        if kj > 0:
            d = kj - 1 - qi
            bias = tile_s[jnp.where(d == 0, 0, jnp.where(d == 1, 1, jnp.where(d == -1, 2, jnp.where(d > 0, 4, 3))))]
            s = s_s[kj - 1] + jnp.concatenate([bias, bias], axis=1)
            s_s[kj - 1] = s
            blk_max = jnp.max(s.reshape(tq // 8, 8, 2 * tq), axis=0)
            mx = blk_max if mx is None else jnp.maximum(mx, blk_max)
    m = jnp.max(mx, axis=0, keepdims=True)
    acc = None
    for kj in range(nk):
        p = jnp.exp2(s_s[kj] - m).astype(BF16)
        pv = _dot(vt_s[:, kj * tq:(kj + 1) * tq], p)
        acc = pv if acc is None else acc + pv
    out = acc[0:LANES] / acc[LANES:LANES + 1]
    att = (out[:, :tq] - lam * out[:, tq:]).T
    ms = jnp.mean(att * att, axis=-1, keepdims=True)
    y = att * lax.rsqrt(ms + A_SUBLN_EPS) * sg_ref[...] * (1.0 - lambda_init)
    g = g_ref[...]
    o_ref[...] = (y * (g * _sigmoid(g))).astype(o_ref.dtype)


def _attn(proj, lam_p, band, subln_g, bsz, seq, bw, layer_idx):
    t = proj.shape[0]
    nb = bw // LANES
    heads = bw // (2 * A_QK)
    tq = ATT_BLOCK
    nq = seq // tq
    lambda_init = 0.8 - 0.6 * math.exp(-0.3 * layer_idx)
    return pl.pallas_call(
        functools.partial(_attn_body, lambda_init=lambda_init),
        grid=(heads, bsz, nq),
        in_specs=[pl.BlockSpec((4, A_QK), lambda h, b, qi: (0, 0)),
                  pl.BlockSpec((tq, LANES), lambda h, b, qi: (b * nq + qi, h)),
                  pl.BlockSpec((seq, LANES), lambda h, b, qi: (b, nb + h)),
                  pl.BlockSpec((seq, LANES), lambda h, b, qi: (b, 2 * nb + h)),
                  pl.BlockSpec((tq, LANES), lambda h, b, qi: (b * nq + qi, 3 * nb + h)),
                  pl.BlockSpec((1, 8, 2 * tq), lambda h, b, qi: (h, 0, 0)),
                  pl.BlockSpec((1, LANES), lambda h, b, qi: (0, 0))],
        out_specs=pl.BlockSpec((tq, LANES), lambda h, b, qi: (b * nq + qi, h)),
        out_shape=jax.ShapeDtypeStruct((t, bw), BF16),
        scratch_shapes=[pltpu.VMEM((5, tq, tq), F32), pltpu.VMEM((seq, LANES), BF16),
                        pltpu.VMEM((LANES + BF16_ROWS, seq), BF16),
                        pltpu.VMEM((seq // tq, tq, 2 * tq), F32)],
        compiler_params=_cparams(("arbitrary", "arbitrary", "arbitrary")),
        name="diff_attn",
    )(lam_p, proj, proj, proj, proj, band, subln_g.reshape(1, LANES))


def _hgrn_consts(c):
    i = np.arange(c)
    lo = (i[:, None] >= i[None, :]).astype(np.float32)
    levels = int(math.log2(c))
    pmask = np.zeros((2, levels + 1, c, c), np.float32)
    pmask[:, 0] = np.eye(c)
    for l in range(levels):
        m = 1 << l
        same = (i[:, None] // (2 * m)) == (i[None, :] // (2 * m))
        second = (i % (2 * m)) >= m
        pmask[0, l + 1] = same & second[:, None] & ~second[None, :]
        pmask[1, l + 1] = same & ~second[:, None] & second[None, :]
    return (jnp.asarray(lo, BF16), jnp.asarray(lo.T, BF16), jnp.asarray(pmask))


def _boundary(cum, m, rev):
    c = cum.shape[0]
    if 2 * m >= 8:
        n = c // (2 * m)
        x = cum.reshape(n, 2 * m, LANES)
        r = m if rev else m - 1
        return jnp.broadcast_to(x[:, r:r + 1, :], x.shape).reshape(c, LANES)
    x = cum.reshape(c // 8, 8, LANES)
    sub = lax.broadcasted_iota(jnp.int32, x.shape, 1)
    out = None
    for node in range(8 // (2 * m)):
        r = node * 2 * m + (m if rev else m - 1)
        bc = jnp.broadcast_to(x[:, r:r + 1, :], x.shape)
        out = bc if out is None else jnp.where(sub >= node * 2 * m, bc, out)
    return out.reshape(c, LANES)


def _hgrn_prep(z, q, v, lbp, tri, pmask_ref, rev):
    lb_floor, one_m_lb, lb_pos = lbp
    c = z.shape[0]
    d = 1 if rev else 0
    e = jnp.exp(-jnp.abs(z))
    r = 1.0 / (1.0 + e)
    pos = z >= 0.0
    sig = jnp.where(pos, r, e * r)
    log_sig = jnp.minimum(z, 0.0) - jnp.log(1.0 + e)
    lf = jnp.where(lb_pos, jnp.log(lb_floor + one_m_lb * sig), log_sig)
    k = one_m_lb * jnp.where(pos, e * r, r)
    cum = _split_dot(tri, lf, 1, 3)
    tot = cum[0:1] if rev else cum[c - 1:c]
    vb = v.astype(BF16)
    scores = _dot_nt(q.astype(BF16), k.astype(BF16)) * pmask_ref[d, 0]
    for l in range(int(math.log2(c))):
        e = jnp.exp(-jnp.abs(cum - _boundary(cum, 1 << l, rev)))
        scores = scores + _dot_nt((q * e).astype(BF16), (k * e).astype(BF16)) * pmask_ref[d, l + 1]
    o = _dot(scores.astype(BF16), vb)
    qd = (q * jnp.exp(cum)).astype(BF16)
    kv_t = _dot(v.T.astype(BF16), (k * jnp.exp(tot - cum)).astype(BF16))
    return o, qd, kv_t, jnp.exp(tot)


def _hgrn_body(q_ref, v_ref, zf_ref, zb_ref, g_ref, lb_ref, ng_ref, lo_ref, up_ref, pmask_ref,
               o_ref, acc_s):
    seq = q_ref.shape[0]
    c = lo_ref.shape[0]
    n = seq // c
    group = C_GROUP
    lbs = lb_ref[...]

    def lb_params(d):
        lb = lbs[d:d + 1]
        return jnp.maximum(lb, LB_FLOOR), 1.0 - lb, lb > 0.0

    lbp = (lb_params(0), lb_params(1))
    tri = (lo_ref[...], up_ref[...])
    z_ref = (zf_ref, zb_ref)
    acc_s[...] = jnp.zeros_like(acc_s)

    def body(i, carry):
        st = list(carry)
        base = (pl.multiple_of(i * (group * c), group * c),
                pl.multiple_of((n // group - 1 - i) * (group * c), group * c))
        prep = []
        for d in range(2):
            for j in range(group):
                rows = pl.ds(base[d] + j * c, c)
                prep.append(_hgrn_prep(z_ref[d][rows, :], q_ref[rows, :], v_ref[rows, :], lbp[d], tri[d],
                                       pmask_ref, d == 1))
        for step in range(group):
            for d in range(2):
                j = step if d == 0 else group - 1 - step
                o, qd, kv_t, dec = prep[d * group + j]
                rows = pl.ds(base[d] + j * c, c)
                acc_s[rows, :] += o + _dot_nt(qd, st[d].astype(BF16))
                st[d] = st[d] * dec + kv_t
        return tuple(st)

    zero = jnp.zeros((LANES, LANES), F32)
    lax.fori_loop(0, n // group, body, (zero, zero))
    for r0 in range(0, seq, ROW_BLOCK):
        rows = pl.ds(r0, ROW_BLOCK)
        o = acc_s[rows, :]
        ms = jnp.mean(o * o, axis=-1, keepdims=True)
        g = g_ref[rows, :]
        o_ref[rows, :] = (o * lax.rsqrt(ms + NORM_EPS) * ng_ref[...] * (g * _sigmoid(g))).astype(o_ref.dtype)


def _hgrn(proj, lb_l, norm_g, bsz, seq, bw):
    t = proj.shape[0]
    nb = bw // LANES
    c0 = 8 * nb + 2
    consts = _hgrn_consts(C_CHUNK)
    col = lambda off: pl.BlockSpec((seq, LANES), lambda b, h: (b, off + h))
    fixed2 = lambda shape: pl.BlockSpec(shape, lambda b, h: (0, 0))
    return pl.pallas_call(
        _hgrn_body,
        grid=(bsz, nb),
        in_specs=[col(c0), col(c0 + nb), col(c0 + 2 * nb), col(c0 + 3 * nb), col(c0 + 4 * nb),
                  pl.BlockSpec((2, LANES), lambda b, h: (0, h)), fixed2((1, LANES)),
                  fixed2(consts[0].shape), fixed2(consts[1].shape),
                  pl.BlockSpec(consts[2].shape, lambda b, h: (0, 0, 0, 0))],
        out_specs=pl.BlockSpec((seq, LANES), lambda b, h: (b, h)),
        out_shape=jax.ShapeDtypeStruct((t, bw), BF16),
        scratch_shapes=[pltpu.VMEM((seq, LANES), F32)],
        compiler_params=_cparams(("parallel", "parallel")),
        name="hgrn2",
    )(proj, proj, proj, proj, proj, lb_l, norm_g.reshape(1, LANES), *consts)


def _rwkv_consts(c, group):
    n = 2 * c
    i = np.arange(n)
    same = (i[:, None] // c) == (i[None, :] // c)
    j = np.arange(group * c)
    lo = ((j[:, None] >= j[None, :]) & (j[:, None] // c == j[None, :] // c)).astype(np.float32)
    tmask = np.stack([(i[:, None] % c > i[None, :] % c), (i[:, None] % c >= i[None, :] % c),
                      (i[:, None] % c < i[None, :] % c), (i[:, None] % c <= i[None, :] % c)]).astype(np.float32)
    levels = int(math.log2(c))
    lvl = np.zeros((levels, n, n), np.float32)
    for l in range(levels):
        m = 1 << l
        lvl[l] = ((i[:, None] // (2 * m)) == (i[None, :] // (2 * m))) & ((i[:, None] // m) != (i[None, :] // m))
    head = (np.arange(LANES)[:, None] // B_HEAD) == (np.arange(LANES)[None, :] // B_HEAD)
    return (jnp.asarray(lo, BF16), jnp.asarray(lo.T, BF16), jnp.asarray(tmask * same[None]), jnp.asarray(lvl),
            jnp.asarray(head.astype(np.float32), BF16), jnp.asarray(np.eye(n, dtype=np.float32)))


def _shifted(ref, mu, t0, rows, seq):
    x = ref[pl.ds(t0, rows), :]
    row = lax.broadcasted_iota(jnp.int32, x.shape, 0)
    if t0 == 0:
        prev = jnp.where(row == 0, 0.0, pltpu.roll(x, 1, 0))
    else:
        prev = ref[pl.ds(t0 - 1, rows), :]
    if t0 + rows == seq:
        nxt = jnp.where(row == rows - 1, 0.0, pltpu.roll(x, rows - 1, 0))
    else:
        nxt = ref[pl.ds(t0 + 1, rows), :]
    return x + mu[0:1] * (prev - x) + mu[1:2] * (nxt - x)


def _stack_heads(x, lane_lo):
    return jnp.concatenate([jnp.where(lane_lo, x, 0.0), jnp.where(lane_lo, 0.0, x)], axis=0)


def _rwkv_prep(r, lw, kk, kka, v, km, tri, tmask_ref, lvl_ref, eye_ref, rev, group):
    c = r.shape[0] // group
    n = 2 * c
    lane_lo = lax.broadcasted_iota(jnp.int32, (c, LANES), 1) < B_HEAD
    cw_all = _split_dot(tri, lw, 1, 3)
    strict = tmask_ref[2 if rev else 0]
    incl = tmask_ref[3 if rev else 1]
    eye = eye_ref[...]
    rows = [slice(i * c, (i + 1) * c) for i in range(group)]
    cw = [cw_all[s] for s in rows]
    tot = [x[0:1] if rev else x[c - 1:c] for x in cw]
    e_neg = [jnp.exp(-x) for x in cw]
    e_tot = [jnp.exp(t - x) for t, x in zip(tot, cw)]
    st = lambda x: _stack_heads(x, lane_lo)
    bd2 = [st(kk[s] * jnp.exp(x - lw[s])) for s, x in zip(rows, cw)]
    ad2 = [st(-(kka[s] * e)) for s, e in zip(rows, e_neg)]
    kd2 = [st(km[s] * e) for s, e in zip(rows, e_neg)]
    rd2 = [st(r[s] * jnp.exp(x)) for s, x in zip(rows, cw)]
    ae2_t = [st(-(kka[s] * e)).T for s, e in zip(rows, e_tot)]
    ke2_t = [st(km[s] * e).T for s, e in zip(rows, e_tot)]
    v2 = [st(v[s]).astype(BF16) for s in rows]
    gram = [_dot_nt(jnp.concatenate([b, q], axis=0).astype(BF16), jnp.concatenate([a, k], axis=0).astype(BF16))
            for b, q, a, k in zip(bd2, rd2, ad2, kd2)]
    ba = [g[:n, :n] * strict for g in gram]
    bk = [g[:n, n:] * strict for g in gram]
    ra = [g[n:, :n] * incl for g in gram]
    rk = [g[n:, n:] * incl for g in gram]
    d = [eye + x * lvl_ref[0] for x in ba]
    for l in range(1, int(math.log2(c))):
        lv = lvl_ref[l]
        db = [x.astype(BF16) for x in d]
        dm = [_dot(x, (a * lv).astype(BF16)).astype(BF16) for x, a in zip(db, ba)]
        d = [x + _dot(y, z) for x, y, z in zip(d, dm, db)]
    xv = [_dot(jnp.concatenate([a, b, k], axis=0).astype(BF16), vv) for a, b, k, vv in zip(bk, rk, ke2_t, v2)]
    tz = [_dot(x.astype(BF16), jnp.concatenate([y[:n], b], axis=1).astype(BF16)) for x, y, b in zip(d, xv, bd2)]
    tr = [jnp.concatenate([t[:, LANES:], q], axis=0).astype(BF16) for t, q in zip(tz, rd2)]
    rae = [jnp.concatenate([a, e], axis=0).astype(BF16) for a, e in zip(ra, ae2_t)]
    dcol = [jnp.broadcast_to(jnp.exp(t), (LANES, LANES)).T for t in tot]
    add = [jnp.concatenate([t[:, :LANES], y[n:], dc], axis=0) for t, y, dc in zip(tz, xv, dcol)]
    return tr, rae, add


def _rwkv_step(tr, rae, add, h):
    n = tr.shape[0] // 2
    p = _dot(tr, h.astype(BF16))
    u2 = add[0:n] + p[:n]
    qq = _dot(rae, u2.astype(BF16))
    y2 = p[n:] + qq[:n] + add[n:2 * n]
    return y2[:n // 2] + y2[n // 2:], add[3 * n:4 * n] * h + qq[n:] + add[2 * n:3 * n]


def _rwkv_body(r_ref, k_ref, v_ref, wd_ref, ad_ref, g_ref, mur_ref, muk_ref, muv_ref, muw_ref, mua_ref,
               w0_ref, a0_ref, wup_ref, aup_ref, kk_ref, ka_ref, rk_ref, lng_ref, lnb_ref,
               lo_ref, up_ref, tmask_ref, lvl_ref, head_ref, eye_ref,
               o_ref, r_s, v_s, kk_s, lw_s, kka_s, km_s, bv_s, y_s, tr_s, rae_s, add_s):
    seq = r_ref.shape[0]
    c = B_CHUNK
    n = seq // c
    rb = ROW_BLOCK
    head = head_ref[...]
    lane_lo = lax.broadcasted_iota(jnp.int32, (rb, LANES), 1) < B_LORA
    for t0 in range(0, seq, rb):
        rows = pl.ds(t0, rb)
        r = _shifted(r_ref, mur_ref[...], t0, rb, seq)
        k = _shifted(k_ref, muk_ref[...], t0, rb, seq)
        v = _shifted(v_ref, muv_ref[...], t0, rb, seq)
        twd = jnp.tanh(_shifted(wd_ref, muw_ref[...], t0, rb, seq))
        ad = _shifted(ad_ref, mua_ref[...], t0, rb, seq)
        kk = k * kk_ref[...]
        kk = kk / jnp.maximum(jnp.sqrt(_split_dot(kk * kk, head, 0, 2)), 1e-12)
        kmsum = None
        for d in range(2):
            sel = lane_lo if d == 0 else jnp.logical_not(lane_lo)
            wl = _dot(jnp.where(sel, twd, 0.0).astype(BF16), wup_ref[...].astype(BF16))
            al = _dot(jnp.where(sel, ad, 0.0).astype(BF16), aup_ref[...].astype(BF16))
            lw_s[d, rows, :] = -math.exp(-0.5) * _sigmoid(w0_ref[d:d + 1, :] + wl)
            a = _sigmoid(a0_ref[d:d + 1, :] + al)
            km = k * (1.0 + (a - 1.0) * ka_ref[...])
            kka_s[d, rows, :] = kk * a
            km_s[d, rows, :] = km
            kmsum = km if kmsum is None else kmsum + km
        bonus = _split_dot(r * kmsum * rk_ref[...], head, 0, 2)
        r_s[rows, :] = r
        v_s[rows, :] = v
        kk_s[rows, :] = kk
        bv_s[rows, :] = bonus * v
    y_s[...] = jnp.zeros_like(y_s)
    tri_lo, tri_up = lo_ref[...], up_ref[...]
    group = B_GROUP
    gc = group * c

    n_it = n // group

    def bases(i):
        return pl.multiple_of(i * gc, gc), pl.multiple_of((n_it - 1 - i) * gc, gc)

    def body(i, carry):
        h_f, h_b = carry
        base_f, base_b = bases(i)
        for d, base, tri in zip((0, 1), (base_f, base_b), (tri_lo, tri_up)):
            rows = pl.ds(base, gc)
            tr, rae, add = _rwkv_prep(r_s[rows, :], lw_s[d, rows, :], kk_s[rows, :], kka_s[d, rows, :], v_s[rows, :],
                                      km_s[d, rows, :], tri, tmask_ref, lvl_ref, eye_ref, d == 1, group)
            for j in range(group):
                tr_s[d, j] = tr[j]
                rae_s[d, j] = rae[j]
                add_s[d, j] = add[j]
        for j in range(group):
            jb = group - 1 - j
            yf, h_f = _rwkv_step(tr_s[0, j], rae_s[0, j], add_s[0, j], h_f)
            yb, h_b = _rwkv_step(tr_s[1, jb], rae_s[1, jb], add_s[1, jb], h_b)
            y_s[pl.ds(base_f + j * c, c), :] += yf
            y_s[pl.ds(base_b + jb * c, c), :] += yb
        return h_f, h_b

    zero = jnp.zeros((LANES, LANES), F32)
    lax.fori_loop(0, n_it, body, (zero, zero))
    inv_n = 1.0 / B_HEAD
    for t0 in range(0, seq, rb):
        rows = pl.ds(t0, rb)
        y = y_s[rows, :]
        mu = _split_dot(y, head, 0, 2) * inv_n
        yc = y - mu
        var = _split_dot(yc * yc, head, 0, 2) * inv_n
        yn = yc * lax.rsqrt(var + B_LNX_EPS) * lng_ref[...] + lnb_ref[...]
        g = g_ref[rows, :]
        o_ref[rows, :] = ((yn + bv_s[rows, :]) * (g * _sigmoid(g))).astype(o_ref.dtype)


def _rwkv(proj, mu, w0, a0, wup, aup, k_k, k_a, r_k, lnx_g, lnx_b, bsz, seq, bw):
    t = proj.shape[0]
    nb = bw // LANES
    c0 = 4 * nb
    consts = _rwkv_consts(B_CHUNK, B_GROUP)
    col = lambda off: pl.BlockSpec((seq, LANES), lambda b, g: (b, off + g))
    colf = lambda off: pl.BlockSpec((seq, LANES), lambda b, g: (b, off))
    par2 = lambda off: pl.BlockSpec((2, LANES), lambda b, g: (0, off + g))
    par2f = lambda off: pl.BlockSpec((2, LANES), lambda b, g: (0, off))
    par1 = pl.BlockSpec((1, LANES), lambda b, g: (0, g))
    up = pl.BlockSpec((2 * B_LORA, LANES), lambda b, g: (0, g))

    def fixed(a):
        return pl.BlockSpec(a.shape, lambda b, g: (0,) * a.ndim)

    row = lambda a: a.reshape(1, bw)
    return pl.pallas_call(
        _rwkv_body,
        grid=(bsz, nb),
        in_specs=[col(c0), col(c0 + nb), col(c0 + 2 * nb), colf(c0 + 3 * nb), colf(c0 + 3 * nb + 1), col(c0 + 3 * nb + 2),
                  par2(0), par2(nb), par2(2 * nb), par2f(3 * nb), par2f(3 * nb + 1),
                  par2(0), par2(0), up, up, par1, par1, par1, par1, par1] + [fixed(a) for a in consts],
        out_specs=pl.BlockSpec((seq, LANES), lambda b, g: (b, g)),
        out_shape=jax.ShapeDtypeStruct((t, bw), BF16),
        scratch_shapes=[pltpu.VMEM((seq, LANES), F32), pltpu.VMEM((seq, LANES), F32), pltpu.VMEM((seq, LANES), F32),
                        pltpu.VMEM((2, seq, LANES), F32), pltpu.VMEM((2, seq, LANES), F32),
                        pltpu.VMEM((2, seq, LANES), F32), pltpu.VMEM((seq, LANES), F32),
                        pltpu.VMEM((seq, LANES), F32),
                        pltpu.VMEM((2, B_GROUP, 4 * B_CHUNK, LANES), BF16),
                        pltpu.VMEM((2, B_GROUP, 4 * B_CHUNK, LANES), BF16),
                        pltpu.VMEM((2, B_GROUP, 8 * B_CHUNK, LANES), F32)],
        compiler_params=_cparams(("parallel", "parallel")),
        name="rwkv7",
    )(proj, proj, proj, proj, proj, proj, mu, mu, mu, mu, mu, w0, a0,
      wup.reshape(2 * B_LORA, bw), aup.reshape(2 * B_LORA, bw),
      row(k_k), row(k_a), row(r_k), row(lnx_g), row(lnx_b), *consts)


def kernel(x, rel_bias, pre_norm_g, post_norm_g, w_in, w_out, lambda_q1, lambda_k1, lambda_q2, lambda_k2, subln_g, rwkv_shift_mu, rwkv_w0, rwkv_w_up, rwkv_a0, rwkv_a_up, rwkv_k_k, rwkv_k_a, rwkv_r_k, rwkv_lnx_g, rwkv_lnx_b, hgrn_lb_logits, hgrn_norm_g):
    bsz, seq, d = x.shape
    depth = w_in.shape[0]
    bw = d // 2
    assert bw % (2 * LANES) == 0 and seq % (2 * ATT_BLOCK) == 0 and seq % (C_GROUP * C_CHUNK) == 0
    assert seq % (B_GROUP * B_CHUNK) == 0
    assert w_in.shape[2] == 13 * bw + 4 * B_LORA
    x2 = x.reshape(bsz * seq, d)
    lb = jax.nn.softmax(hgrn_lb_logits.astype(F32), axis=1)
    lb = jnp.cumsum(lb, axis=1) - lb[:, :1]
    band = _bias_band(rel_bias, ATT_BLOCK)
    for l in range(depth):
        proj = _proj(x2, pre_norm_g[l], w_in[l].astype(BF16))
        lam_p = jnp.stack([lambda_q1[l], lambda_k1[l], lambda_q2[l], lambda_k2[l]]).astype(F32)
        ya = _attn(proj, lam_p, band, subln_g[l], bsz, seq, bw, l)
        yb = _rwkv(proj, rwkv_shift_mu[l], rwkv_w0[l], rwkv_a0[l], rwkv_w_up[l], rwkv_a_up[l], rwkv_k_k[l],
                   rwkv_k_a[l], rwkv_r_k[l], rwkv_lnx_g[l], rwkv_lnx_b[l], bsz, seq, bw)
        yc = _hgrn(proj, lb[:, l], hgrn_norm_g[l], bsz, seq, bw)
        x2 = _out(ya, yb, yc, w_out[l].astype(BF16), post_norm_g[l], x2)
    return x2.reshape(bsz, seq, d)
```

```python
import functools
import math

import numpy as np
import jax
import jax.numpy as jnp
from jax import lax
from jax.experimental import pallas as pl
from jax.experimental.pallas import tpu as pltpu

F32 = jnp.float32
BF16 = jnp.bfloat16

LANES = 128
BF16_ROWS = 16
VMEM_LIMIT = 52 * 1024 * 1024

LOG2E = math.log2(math.e)

NORM_EPS = 1e-6
A_QK = 64
A_SUBLN_EPS = 1e-5
NUM_BUCKETS = 32
MAX_DISTANCE = 128
ATT_BLOCK = 512
B_HEAD = 64
B_LORA = 64
B_LNX_EPS = 64e-5
B_CHUNK = 64
B_GROUP = 8
B_CUM = 4
C_CHUNK = 128
C_GROUP = 4
LB_FLOOR = 1e-30
ROW_BLOCK = 256

_NT = (((1,), (1,)), ((), ()))
_TN = (((0,), (0,)), ((), ()))


def _dot(a, b):
    return jnp.dot(a, b, preferred_element_type=F32)


def _dot_nt(a, b):
    return lax.dot_general(a, b, _NT, preferred_element_type=F32)


def _split_dot(a, b, split, passes):
    rem = a if split == 0 else b
    acc = None
    for _ in range(passes):
        piece = rem.astype(BF16)
        d = _dot(piece, b) if split == 0 else _dot(a, piece)
        acc = d if acc is None else acc + d
        rem = rem - piece.astype(F32)
    return acc


def _sigmoid(x):
    return 1.0 / (1.0 + jnp.exp(-x))


def _cparams(sem):
    return pltpu.CompilerParams(dimension_semantics=sem, vmem_limit_bytes=VMEM_LIMIT)


def _proj_body(x_ref, g_ref, w_ref, o_ref, u_s):
    @pl.when(pl.program_id(1) == 0)
    def _():
        xf = x_ref[...]
        ms = jnp.mean(xf * xf, axis=-1, keepdims=True)
        u_s[...] = (xf * lax.rsqrt(ms + NORM_EPS) * g_ref[...]).astype(BF16)

    o_ref[...] = _dot(u_s[...], w_ref[...])


def _proj(x2, g, w, tm=1024, tn=1024):
    t, d = x2.shape
    p = w.shape[1]
    return pl.pallas_call(
        _proj_body,
        grid=(t // tm, pl.cdiv(p, tn)),
        in_specs=[pl.BlockSpec((tm, d), lambda i, j: (i, 0)),
                  pl.BlockSpec((1, d), lambda i, j: (0, 0)),
                  pl.BlockSpec((d, tn), lambda i, j: (0, j))],
        out_specs=pl.BlockSpec((tm, tn), lambda i, j: (i, j)),
        out_shape=jax.ShapeDtypeStruct((t, p), F32),
        scratch_shapes=[pltpu.VMEM((tm, d), BF16)],
        compiler_params=_cparams(("parallel", "arbitrary")),
        name="proj_in",
    )(x2, g.reshape(1, d), w)


def _out_body(ya_ref, yb_ref, yc_ref, w_ref, g_ref, x_ref, o_ref):
    bw = ya_ref.shape[1]
    m = (_dot(ya_ref[...], w_ref[0:bw, :]) + _dot(yb_ref[...], w_ref[bw:2 * bw, :])
         + _dot(yc_ref[...], w_ref[2 * bw:3 * bw, :]))
    ms = jnp.mean(m * m, axis=-1, keepdims=True)
    o_ref[...] = x_ref[...] + m * lax.rsqrt(ms + NORM_EPS) * g_ref[...]


def _out(ya, yb, yc, w, g, x2, tm=256):
    t, d = x2.shape
    bw = ya.shape[1]
    row = lambda i: (i, 0)
    fixed = lambda i: (0, 0)
    return pl.pallas_call(
        _out_body,
        grid=(t // tm,),
        in_specs=[pl.BlockSpec((tm, bw), row), pl.BlockSpec((tm, bw), row), pl.BlockSpec((tm, bw), row),
                  pl.BlockSpec((3 * bw, d), fixed), pl.BlockSpec((1, d), fixed), pl.BlockSpec((tm, d), row)],
        out_specs=pl.BlockSpec((tm, d), row),
        out_shape=jax.ShapeDtypeStruct((t, d), F32),
        compiler_params=_cparams(("parallel",)),
        name="proj_out",
    )(ya, yb, yc, w, g.reshape(1, d), x2)


def _t5_bucket(rel):
    half = NUM_BUCKETS // 2
    max_exact = half // 2
    n = jnp.abs(rel)
    nf = jnp.maximum(n, max_exact).astype(F32)
    large = max_exact + (jnp.log(nf / max_exact) / math.log(MAX_DISTANCE / max_exact)
                         * (half - max_exact)).astype(jnp.int32)
    large = jnp.minimum(large, half - 1)
    return jnp.where(rel > 0, half, 0) + jnp.where(n < max_exact, n, large)


def _bias_band(rel_bias, blk):
    c = blk - jnp.arange(2 * blk, dtype=jnp.int32)
    far = jnp.full((2 * blk,), 2 * blk, jnp.int32)
    rel = jnp.stack([c, c + blk, c - blk, -far, far, far, far, far])
    return jnp.transpose(rel_bias.astype(F32)[_t5_bucket(rel)], (2, 0, 1))


def _attn_body(lam_ref, q_ref, k_ref, v_ref, g_ref, band_ref, sg_ref, o_ref, tile_s, kb_s, vt_s, s_s, *,
               lambda_init):
    b = pl.program_id(1)
    qi = pl.program_id(2)
    tq = q_ref.shape[0]
    nk = k_ref.shape[0] // tq

    @pl.when((b == 0) & (qi == 0))
    def _():
        vec = band_ref[0] * LOG2E
        for d in range(3):
            w = jnp.broadcast_to(vec[d:d + 1, :], (tq, 2 * tq))
            tile_s[d] = pltpu.roll(w, tq, 1, stride=1, stride_axis=0)[:, :tq]
        for d in range(3, 5):
            tile_s[d] = jnp.broadcast_to(vec[d:d + 1, :tq], (tq, tq))

    @pl.when(qi == 0)
    def _():
        row = lax.broadcasted_iota(jnp.int32, (BF16_ROWS, k_ref.shape[0]), 0)
        vt_s[LANES:LANES + BF16_ROWS, :] = jnp.where(row == 0, 1.0, 0.0).astype(BF16)
        for kj in range(nk):
            rows = pl.ds(kj * tq, tq)
            kb_s[rows, :] = k_ref[rows, :].astype(BF16)
            vt_s[0:LANES, kj * tq:(kj + 1) * tq] = v_ref[rows, :].T.astype(BF16)

    lp = lam_ref[...]
    lam = (jnp.exp(jnp.sum(lp[0:1] * lp[1:2], axis=-1, keepdims=True))
           - jnp.exp(jnp.sum(lp[2:3] * lp[3:4], axis=-1, keepdims=True)) + lambda_init)

    q = q_ref[...] * (A_QK ** -0.5 * LOG2E)
    lane = lax.broadcasted_iota(jnp.int32, q.shape, 1)
    q2 = jnp.concatenate([jnp.where(lane < A_QK, q, 0.0), jnp.where(lane >= A_QK, q, 0.0)], axis=0).astype(BF16)
    mx = None
    for kj in range(nk + 1):
        if kj < nk:
            s_s[kj] = _dot_nt(kb_s[pl.ds(kj * tq, tq), :], q2)
        if kj > 0:
            d = kj - 1 - qi
            bias = tile_s[jnp.where(d == 0, 0, jnp.where(d == 1, 1, jnp.where(d == -1, 2, jnp.where(d > 0, 4, 3))))]
            s = s_s[kj - 1] + jnp.concatenate([bias, bias], axis=1)
            s_s[kj - 1] = s
            blk_max = jnp.max(s.reshape(tq // 8, 8, 2 * tq), axis=0)
            mx = blk_max if mx is None else jnp.maximum(mx, blk_max)
    m = jnp.max(mx, axis=0, keepdims=True)
    acc = None
    for kj in range(nk):
        p = jnp.exp2(s_s[kj] - m).astype(BF16)
        pv = _dot(vt_s[:, kj * tq:(kj + 1) * tq], p)
        acc = pv if acc is None else acc + pv
    out = acc[0:LANES] / acc[LANES:LANES + 1]
    att = (out[:, :tq] - lam * out[:, tq:]).T
    ms = jnp.mean(att * att, axis=-1, keepdims=True)
    y = att * lax.rsqrt(ms + A_SUBLN_EPS) * sg_ref[...] * (1.0 - lambda_init)
    g = g_ref[...]
    o_ref[...] = (y * (g * _sigmoid(g))).astype(o_ref.dtype)


def _attn(proj, lam_p, band, subln_g, bsz, seq, bw, layer_idx):
    t = proj.shape[0]
    nb = bw // LANES
    heads = bw // (2 * A_QK)
    tq = ATT_BLOCK
    nq = seq // tq
    lambda_init = 0.8 - 0.6 * math.exp(-0.3 * layer_idx)
    return pl.pallas_call(
        functools.partial(_attn_body, lambda_init=lambda_init),
        grid=(heads, bsz, nq),
        in_specs=[pl.BlockSpec((4, A_QK), lambda h, b, qi: (0, 0)),
                  pl.BlockSpec((tq, LANES), lambda h, b, qi: (b * nq + qi, h)),
                  pl.BlockSpec((seq, LANES), lambda h, b, qi: (b, nb + h)),
                  pl.BlockSpec((seq, LANES), lambda h, b, qi: (b, 2 * nb + h)),
                  pl.BlockSpec((tq, LANES), lambda h, b, qi: (b * nq + qi, 3 * nb + h)),
                  pl.BlockSpec((1, 8, 2 * tq), lambda h, b, qi: (h, 0, 0)),
                  pl.BlockSpec((1, LANES), lambda h, b, qi: (0, 0))],
        out_specs=pl.BlockSpec((tq, LANES), lambda h, b, qi: (b * nq + qi, h)),
        out_shape=jax.ShapeDtypeStruct((t, bw), BF16),
        scratch_shapes=[pltpu.VMEM((5, tq, tq), F32), pltpu.VMEM((seq, LANES), BF16),
                        pltpu.VMEM((LANES + BF16_ROWS, seq), BF16),
                        pltpu.VMEM((seq // tq, tq, 2 * tq), F32)],
        compiler_params=_cparams(("arbitrary", "arbitrary", "arbitrary")),
        name="diff_attn",
    )(lam_p, proj, proj, proj, proj, band, subln_g.reshape(1, LANES))


def _hgrn_consts(c):
    i = np.arange(c)
    lo = (i[:, None] >= i[None, :]).astype(np.float32)
    levels = int(math.log2(c))
    pmask = np.zeros((2, levels + 1, c, c), np.float32)
    pmask[:, 0] = np.eye(c)
    for l in range(levels):
        m = 1 << l
        same = (i[:, None] // (2 * m)) == (i[None, :] // (2 * m))
        second = (i % (2 * m)) >= m
        pmask[0, l + 1] = same & second[:, None] & ~second[None, :]
        pmask[1, l + 1] = same & ~second[:, None] & second[None, :]
    return (jnp.asarray(lo, BF16), jnp.asarray(lo.T, BF16), jnp.asarray(pmask))


def _boundary(cum, m, rev):
    c = cum.shape[0]
    if 2 * m >= 8:
        n = c // (2 * m)
        x = cum.reshape(n, 2 * m, LANES)
        r = m if rev else m - 1
        return jnp.broadcast_to(x[:, r:r + 1, :], x.shape).reshape(c, LANES)
    x = cum.reshape(c // 8, 8, LANES)
    sub = lax.broadcasted_iota(jnp.int32, x.shape, 1)
    out = None
    for node in range(8 // (2 * m)):
        r = node * 2 * m + (m if rev else m - 1)
        bc = jnp.broadcast_to(x[:, r:r + 1, :], x.shape)
        out = bc if out is None else jnp.where(sub >= node * 2 * m, bc, out)
    return out.reshape(c, LANES)


def _hgrn_prep(z, q, v, lbp, tri, pmask_ref, rev):
    lb_floor, one_m_lb, lb_pos = lbp
    c = z.shape[0]
    d = 1 if rev else 0
    e = jnp.exp(-jnp.abs(z))
    r = 1.0 / (1.0 + e)
    pos = z >= 0.0
    sig = jnp.where(pos, r, e * r)
    log_sig = jnp.minimum(z, 0.0) - jnp.log(1.0 + e)
    lf = jnp.where(lb_pos, jnp.log(lb_floor + one_m_lb * sig), log_sig)
    k = one_m_lb * jnp.where(pos, e * r, r)
    cum = _split_dot(tri, lf, 1, 3)
    tot = cum[0:1] if rev else cum[c - 1:c]
    vb = v.astype(BF16)
    scores = _dot_nt(q.astype(BF16), k.astype(BF16)) * pmask_ref[d, 0]
    for l in range(int(math.log2(c))):
        e = jnp.exp(-jnp.abs(cum - _boundary(cum, 1 << l, rev)))
        scores = scores + _dot_nt((q * e).astype(BF16), (k * e).astype(BF16)) * pmask_ref[d, l + 1]
    o = _dot(scores.astype(BF16), vb)
    qd = (q * jnp.exp(cum)).astype(BF16)
    kv_t = _dot(v.T.astype(BF16), (k * jnp.exp(tot - cum)).astype(BF16))
    return o, qd, kv_t, jnp.exp(tot)


N_HGRN_IN = 10


def _hgrn_parts(q_ref, v_ref, zf_ref, zb_ref, g_ref, lb_ref, ng_ref, lo_ref, up_ref, pmask_ref,
                o_ref, acc_s):
    seq = q_ref.shape[0]
    c = lo_ref.shape[0]
    n = seq // c
    group = C_GROUP
    lbs = lb_ref[...]

    def lb_params(d):
        lb = lbs[d:d + 1]
        return jnp.maximum(lb, LB_FLOOR), 1.0 - lb, lb > 0.0

    lbp = (lb_params(0), lb_params(1))
    tri = (lo_ref[...], up_ref[...])
    z_ref = (zf_ref, zb_ref)
    acc_s[...] = jnp.zeros_like(acc_s)

    def body(i, carry):
        st = list(carry)
        base = (pl.multiple_of(i * (group * c), group * c),
                pl.multiple_of((n // group - 1 - i) * (group * c), group * c))
        prep = []
        for d in range(2):
            for j in range(group):
                rows = pl.ds(base[d] + j * c, c)
                prep.append(_hgrn_prep(z_ref[d][rows, :], q_ref[rows, :], v_ref[rows, :], lbp[d], tri[d],
                                       pmask_ref, d == 1))
        for step in range(group):
            for d in range(2):
                j = step if d == 0 else group - 1 - step
                o, qd, kv_t, dec = prep[d * group + j]
                rows = pl.ds(base[d] + j * c, c)
                acc_s[rows, :] += o + _dot_nt(qd, st[d].astype(BF16))
                st[d] = st[d] * dec + kv_t
        return tuple(st)

    def finish():
        for r0 in range(0, seq, ROW_BLOCK):
            rows = pl.ds(r0, ROW_BLOCK)
            o = acc_s[rows, :]
            ms = jnp.mean(o * o, axis=-1, keepdims=True)
            g = g_ref[rows, :]
            o_ref[rows, :] = (o * lax.rsqrt(ms + NORM_EPS) * ng_ref[...] * (g * _sigmoid(g))).astype(o_ref.dtype)

    zero = jnp.zeros((LANES, LANES), F32)
    return n // group, (zero, zero), body, finish


def _hgrn_operands(proj, lb_l, norm_g, seq, bw):
    nb = bw // LANES
    c0 = 8 * nb + 2
    consts = _hgrn_consts(C_CHUNK)
    col = lambda off: pl.BlockSpec((seq, LANES), lambda b, h: (b, off + h))
    fixed2 = lambda shape: pl.BlockSpec(shape, lambda b, h: (0, 0))
    in_specs = [col(c0), col(c0 + nb), col(c0 + 2 * nb), col(c0 + 3 * nb), col(c0 + 4 * nb),
                pl.BlockSpec((2, LANES), lambda b, h: (0, h)), fixed2((1, LANES)),
                fixed2(consts[0].shape), fixed2(consts[1].shape),
                pl.BlockSpec(consts[2].shape, lambda b, h: (0, 0, 0, 0))]
    operands = (proj, proj, proj, proj, proj, lb_l, norm_g.reshape(1, LANES), *consts)
    assert len(operands) == N_HGRN_IN
    return in_specs, operands, [pltpu.VMEM((seq, LANES), F32)]


def _rwkv_consts(c, group):
    i = np.arange(c)
    j = np.arange(group * c)
    lo = ((j[:, None] >= j[None, :]) & (j[:, None] // c == j[None, :] // c)).astype(np.float32)
    tmask = np.stack([i[:, None] > i[None, :], i[:, None] >= i[None, :],
                      i[:, None] < i[None, :], i[:, None] <= i[None, :]]).astype(np.float32)
    levels = int(math.log2(c))
    lvl = np.zeros((levels, c, c), np.float32)
    for l in range(levels):
        m = 1 << l
        lvl[l] = ((i[:, None] // (2 * m)) == (i[None, :] // (2 * m))) & ((i[:, None] // m) != (i[None, :] // m))
    head = (np.arange(LANES)[:, None] // B_HEAD) == (np.arange(LANES)[None, :] // B_HEAD)
    tile4 = lambda a: np.concatenate([a] * 4, axis=-1)
    quarter = np.stack([np.broadcast_to((np.arange(4 * c) // c) == q, (c, 4 * c)) for q in range(4)])
    return (jnp.asarray(lo, BF16), jnp.asarray(lo.T, BF16), jnp.asarray(tile4(tmask)), jnp.asarray(tile4(lvl)),
            jnp.asarray(head.astype(np.float32), BF16), jnp.asarray(tile4(np.eye(c, dtype=np.float32))),
            jnp.asarray(quarter.astype(np.float32), BF16))


def _shifted(ref, mu, t0, rows, seq):
    x = ref[pl.ds(t0, rows), :]
    row = lax.broadcasted_iota(jnp.int32, x.shape, 0)
    if t0 == 0:
        prev = jnp.where(row == 0, 0.0, pltpu.roll(x, 1, 0))
    else:
        prev = ref[pl.ds(t0 - 1, rows), :]
    if t0 + rows == seq:
        nxt = jnp.where(row == rows - 1, 0.0, pltpu.roll(x, rows - 1, 0))
    else:
        nxt = ref[pl.ds(t0 + 1, rows), :]
    return x + mu[0:1] * (prev - x) + mu[1:2] * (nxt - x)


def _stack_heads(x, lane_lo):
    return jnp.concatenate([jnp.where(lane_lo, x, 0.0), jnp.where(lane_lo, 0.0, x)], axis=0)


def _rwkv_prep(r, lw, kk, kka, v, km, tri, tmask_ref, lvl_ref, eye_ref, quarter_ref, hmask, rev, group):
    c = r.shape[0] // group
    slab = tri.shape[0]
    lane_lo = lax.broadcasted_iota(jnp.int32, (c, LANES), 1) < B_HEAD
    row_lo = lax.broadcasted_iota(jnp.int32, (LANES, c), 0) < B_HEAD
    cw_all = jnp.concatenate([_split_dot(tri, lw[i:i + slab], 1, 3) for i in range(0, group * c, slab)], axis=0)
    strict = tmask_ref[2 if rev else 0]
    incl = tmask_ref[3 if rev else 1]
    eye = eye_ref[...]
    rows = [slice(i * c, (i + 1) * c) for i in range(group)]
    cw = [cw_all[s] for s in rows]
    tot = [x[0:1] if rev else x[c - 1:c] for x in cw]
    e_neg = [jnp.exp(-x) for x in cw]
    e_tot = [jnp.exp(t - x) for t, x in zip(tot, cw)]
    heads = lambda x: _stack_heads(x, lane_lo)
    bd = [kk[s] * jnp.exp(x - lw[s]) for s, x in zip(rows, cw)]
    ad = [-(kka[s] * e) for s, e in zip(rows, e_neg)]
    kd = [km[s] * e for s, e in zip(rows, e_neg)]
    rd = [r[s] * jnp.exp(x) for s, x in zip(rows, cw)]
    ae_t = [(-(kka[s] * e)).T for s, e in zip(rows, e_tot)]
    ke_t = [(km[s] * e).T.astype(BF16) for s, e in zip(rows, e_tot)]
    v2 = [heads(v[s]).astype(BF16) for s in rows]
    dcol = [jnp.broadcast_to(jnp.exp(t), (LANES, LANES)).T for t in tot]
    kv = [_dot(k, v[s].astype(BF16)) * hmask for k, s in zip(ke_t, rows)]
    pairs = [(2 * i, 2 * i + 1) for i in range(group // 2)]
    side = lambda xs: [jnp.concatenate([xs[a], xs[b]], axis=1) for a, b in pairs]
    lhs = side([jnp.concatenate([b, q], axis=0).astype(BF16) for b, q in zip(bd, rd)])
    rhs = [jnp.concatenate([heads(ad[ch]), heads(kd[ch])], axis=0).astype(BF16) for ch in range(group)]
    gram = [_dot_nt(x, _block_diag(rhs[a], rhs[b])) for x, (a, b) in zip(lhs, pairs)]
    pick = lambda g, i, j: jnp.concatenate([g[i * c:(i + 1) * c, j * LANES:(j + 1) * LANES],
                                            g[i * c:(i + 1) * c, (2 + j) * LANES:(3 + j) * LANES]], axis=1)
    ba = [pick(g, 0, 0) * strict for g in gram]
    bk = [pick(g, 0, 1) * strict for g in gram]
    ra = [pick(g, 1, 0) * incl for g in gram]
    rk = [pick(g, 1, 1) * incl for g in gram]
    spread = lambda x: jnp.concatenate([x * quarter_ref[q] for q in range(4)], axis=0)
    d = [eye + x * lvl_ref[0] for x in ba]
    for l in range(1, int(math.log2(c))):
        lv = lvl_ref[l]
        db = [x.astype(BF16) for x in d]
        dm = [_dot(x, spread((a * lv).astype(BF16))).astype(BF16) for x, a in zip(db, ba)]
        d = [x + _dot(y, spread(z)) for x, y, z in zip(d, dm, db)]
    xv = [_dot(jnp.concatenate([x, y], axis=0).astype(BF16), _block_diag(v2[a], v2[b]))
          for x, y, (a, b) in zip(bk, rk, pairs)]
    zrows = lambda y, ch: heads2_wide(jnp.concatenate([y, bd[ch]], axis=1)).astype(BF16)
    lane_lo2 = jnp.concatenate([lane_lo, lane_lo], axis=1)
    heads2_wide = lambda z: jnp.concatenate([jnp.where(lane_lo2, z, 0.0), jnp.where(lane_lo2, 0.0, z)], axis=0)
    tz = [_dot(x.astype(BF16), _block_diag(zrows(y[:c, :LANES], a), zrows(y[:c, LANES:], b)))
          for x, y, (a, b) in zip(d, xv, pairs)]
    tr, rae, add = [], [], []
    for i, (a, b) in enumerate(pairs):
        for h, ch in enumerate((a, b)):
            t = tz[i][:, 2 * h * LANES:(2 * h + 2) * LANES]
            y = xv[i][:, h * LANES:(h + 1) * LANES]
            aet = jnp.concatenate([jnp.where(row_lo, ae_t[ch], 0.0), jnp.where(row_lo, 0.0, ae_t[ch])], axis=1)
            tr.append(jnp.concatenate([t[:, LANES:], rd[ch]], axis=0).astype(BF16))
            rae.append(jnp.concatenate([ra[i][:, h * LANES:(h + 1) * LANES], aet], axis=0).astype(BF16))
            add.append(jnp.concatenate([t[:, :LANES], y[c:], kv[ch], dcol[ch]], axis=0))
    return tr, rae, add


def _block_diag(x, y):
    z = jnp.zeros_like(x)
    return jnp.concatenate([jnp.concatenate([x, z], axis=1), jnp.concatenate([z, y], axis=1)], axis=0)


def _rwkv_step(tr, rae, add, h):
    c = tr[0].shape[0] // 2
    lane_lo = lax.broadcasted_iota(jnp.int32, (c, LANES), 1) < B_HEAD
    side = lambda f, b: jnp.concatenate([f, b], axis=1)
    lanes = lambda x, d: x[:, d * LANES:(d + 1) * LANES]
    hb = h.astype(BF16)
    p = _dot(side(*tr), _block_diag(hb[:, :LANES], hb[:, LANES:]))
    u = [_stack_heads(add[d][0:c] + lanes(p[:c], d), lane_lo).astype(BF16) for d in range(2)]
    qq = _dot(side(*rae), _block_diag(*u))
    ys = [lanes(p[c:], d) + lanes(qq[:c], d) + add[d][c:2 * c] for d in range(2)]
    kv, dec = slice(2 * c, 2 * c + LANES), slice(2 * c + LANES, 2 * c + 2 * LANES)
    h = side(add[0][dec], add[1][dec]) * h + qq[c:] + side(add[0][kv], add[1][kv])
    return ys, h


N_RWKV_IN = 27
N_RWKV_SCRATCH = 11


def _rwkv_parts(r_ref, k_ref, v_ref, wd_ref, ad_ref, g_ref, mur_ref, muk_ref, muv_ref, muw_ref, mua_ref,
                w0_ref, a0_ref, wup_ref, aup_ref, kk_ref, ka_ref, rk_ref, lng_ref, lnb_ref,
                lo_ref, up_ref, tmask_ref, lvl_ref, head_ref, eye_ref, quarter_ref,
                o_ref, r_s, v_s, kk_s, lw_s, kka_s, km_s, bv_s, y_s, tr_s, rae_s, add_s):
    seq = r_ref.shape[0]
    c = B_CHUNK
    n = seq // c
    rb = ROW_BLOCK
    head = head_ref[...]
    hmask = head.astype(F32)
    lane_lo = lax.broadcasted_iota(jnp.int32, (rb, LANES), 1) < B_LORA
    for t0 in range(0, seq, rb):
        rows = pl.ds(t0, rb)
        r = _shifted(r_ref, mur_ref[...], t0, rb, seq)
        k = _shifted(k_ref, muk_ref[...], t0, rb, seq)
        v = _shifted(v_ref, muv_ref[...], t0, rb, seq)
        twd = jnp.tanh(_shifted(wd_ref, muw_ref[...], t0, rb, seq))
        ad = _shifted(ad_ref, mua_ref[...], t0, rb, seq)
        kk = k * kk_ref[...]
        kk = kk / jnp.maximum(jnp.sqrt(_split_dot(kk * kk, head, 0, 2)), 1e-12)
        kmsum = None
        for d in range(2):
            sel = lane_lo if d == 0 else jnp.logical_not(lane_lo)
            wl = _dot(jnp.where(sel, twd, 0.0).astype(BF16), wup_ref[...].astype(BF16))
            al = _dot(jnp.where(sel, ad, 0.0).astype(BF16), aup_ref[...].astype(BF16))
            lw_s[d, rows, :] = -math.exp(-0.5) * _sigmoid(w0_ref[d:d + 1, :] + wl)
            a = _sigmoid(a0_ref[d:d + 1, :] + al)
            km = k * (1.0 + (a - 1.0) * ka_ref[...])
            kka_s[d, rows, :] = kk * a
            km_s[d, rows, :] = km
            kmsum = km if kmsum is None else kmsum + km
        bonus = _split_dot(r * kmsum * rk_ref[...], head, 0, 2)
        r_s[rows, :] = r
        v_s[rows, :] = v
        kk_s[rows, :] = kk
        bv_s[rows, :] = bonus * v
    y_s[...] = jnp.zeros_like(y_s)
    tri_lo, tri_up = lo_ref[...], up_ref[...]
    group = B_GROUP
    gc = group * c

    n_it = n // group

    def bases(i):
        return pl.multiple_of(i * gc, gc), pl.multiple_of((n_it - 1 - i) * gc, gc)

    def body(i, h):
        base_f, base_b = bases(i)
        for d, base, tri in zip((0, 1), (base_f, base_b), (tri_lo, tri_up)):
            rows = pl.ds(base, gc)
            tr, rae, add = _rwkv_prep(r_s[rows, :], lw_s[d, rows, :], kk_s[rows, :], kka_s[d, rows, :], v_s[rows, :],
                                      km_s[d, rows, :], tri, tmask_ref, lvl_ref, eye_ref, quarter_ref, hmask,
                                      d == 1, group)
            for j in range(group):
                tr_s[d, j] = tr[j]
                rae_s[d, j] = rae[j]
                add_s[d, j] = add[j]
        for j in range(group):
            jb = group - 1 - j
            (yf, yb), h = _rwkv_step((tr_s[0, j], tr_s[1, jb]), (rae_s[0, j], rae_s[1, jb]),
                                     (add_s[0, j], add_s[1, jb]), h)
            y_s[pl.ds(base_f + j * c, c), :] += yf
            y_s[pl.ds(base_b + jb * c, c), :] += yb
        return h

    def finish():
        inv_n = 1.0 / B_HEAD
        for t0 in range(0, seq, rb):
            rows = pl.ds(t0, rb)
            y = y_s[rows, :]
            mu = _split_dot(y, head, 0, 2) * inv_n
            yc = y - mu
            var = _split_dot(yc * yc, head, 0, 2) * inv_n
            yn = yc * lax.rsqrt(var + B_LNX_EPS) * lng_ref[...] + lnb_ref[...]
            g = g_ref[rows, :]
            o_ref[rows, :] = ((yn + bv_s[rows, :]) * (g * _sigmoid(g))).astype(o_ref.dtype)

    return n_it, jnp.zeros((LANES, 2 * LANES), F32), body, finish


def _rwkv_operands(proj, mu, w0, a0, wup, aup, k_k, k_a, r_k, lnx_g, lnx_b, seq, bw):
    nb = bw // LANES
    c0 = 4 * nb
    consts = _rwkv_consts(B_CHUNK, B_CUM)
    col = lambda off: pl.BlockSpec((seq, LANES), lambda b, g: (b, off + g))
    colf = lambda off: pl.BlockSpec((seq, LANES), lambda b, g: (b, off))
    par2 = lambda off: pl.BlockSpec((2, LANES), lambda b, g: (0, off + g))
    par2f = lambda off: pl.BlockSpec((2, LANES), lambda b, g: (0, off))
    par1 = pl.BlockSpec((1, LANES), lambda b, g: (0, g))
    up = pl.BlockSpec((2 * B_LORA, LANES), lambda b, g: (0, g))

    def fixed(a):
        return pl.BlockSpec(a.shape, lambda b, g: (0,) * a.ndim)

    row = lambda a: a.reshape(1, bw)
    in_specs = [col(c0), col(c0 + nb), col(c0 + 2 * nb), colf(c0 + 3 * nb), colf(c0 + 3 * nb + 1), col(c0 + 3 * nb + 2),
                par2(0), par2(nb), par2(2 * nb), par2f(3 * nb), par2f(3 * nb + 1),
                par2(0), par2(0), up, up, par1, par1, par1, par1, par1] + [fixed(a) for a in consts]
    operands = (proj, proj, proj, proj, proj, proj, mu, mu, mu, mu, mu, w0, a0,
                wup.reshape(2 * B_LORA, bw), aup.reshape(2 * B_LORA, bw),
                row(k_k), row(k_a), row(r_k), row(lnx_g), row(lnx_b), *consts)
    scratch = [pltpu.VMEM((seq, LANES), F32), pltpu.VMEM((seq, LANES), F32), pltpu.VMEM((seq, LANES), F32),
               pltpu.VMEM((2, seq, LANES), F32), pltpu.VMEM((2, seq, LANES), F32),
               pltpu.VMEM((2, seq, LANES), F32), pltpu.VMEM((seq, LANES), F32),
               pltpu.VMEM((seq, LANES), F32),
               pltpu.VMEM((2, B_GROUP, 2 * B_CHUNK, LANES), BF16),
               pltpu.VMEM((2, B_GROUP, B_CHUNK + LANES, LANES), BF16),
               pltpu.VMEM((2, B_GROUP, 2 * B_CHUNK + 2 * LANES, LANES), F32)]
    assert len(operands) == N_RWKV_IN and len(scratch) == N_RWKV_SCRATCH
    return in_specs, operands, scratch


def _recurrent_body(*refs):
    rwkv_in = refs[:N_RWKV_IN]
    hgrn_in = refs[N_RWKV_IN:N_RWKV_IN + N_HGRN_IN]
    rwkv_out, hgrn_out = refs[N_RWKV_IN + N_HGRN_IN:N_RWKV_IN + N_HGRN_IN + 2]
    scratch = refs[N_RWKV_IN + N_HGRN_IN + 2:]
    trips_b, carry_b, body_b, finish_b = _rwkv_parts(*rwkv_in, rwkv_out, *scratch[:N_RWKV_SCRATCH])
    trips_c, carry_c, body_c, finish_c = _hgrn_parts(*hgrn_in, hgrn_out, *scratch[N_RWKV_SCRATCH:])
    assert trips_b == trips_c

    def body(i, carry):
        return body_b(i, carry[0]), body_c(i, carry[1])

    lax.fori_loop(0, trips_b, body, (carry_b, carry_c))
    finish_b()
    finish_c()


def _recurrent(proj, rwkv_params, hgrn_params, bsz, seq, bw):
    t = proj.shape[0]
    nb = bw // LANES
    specs_b, ops_b, scratch_b = _rwkv_operands(proj, *rwkv_params, seq, bw)
    specs_c, ops_c, scratch_c = _hgrn_operands(proj, *hgrn_params, seq, bw)
    out_spec = pl.BlockSpec((seq, LANES), lambda b, g: (b, g))
    out_shape = jax.ShapeDtypeStruct((t, bw), BF16)
    return pl.pallas_call(
        _recurrent_body,
        grid=(bsz, nb),
        in_specs=specs_b + specs_c,
        out_specs=[out_spec, out_spec],
        out_shape=[out_shape, out_shape],
        scratch_shapes=scratch_b + scratch_c,
        compiler_params=_cparams(("parallel", "parallel")),
        name="rwkv7_hgrn2",
    )(*ops_b, *ops_c)


def kernel(x, rel_bias, pre_norm_g, post_norm_g, w_in, w_out, lambda_q1, lambda_k1, lambda_q2, lambda_k2, subln_g, rwkv_shift_mu, rwkv_w0, rwkv_w_up, rwkv_a0, rwkv_a_up, rwkv_k_k, rwkv_k_a, rwkv_r_k, rwkv_lnx_g, rwkv_lnx_b, hgrn_lb_logits, hgrn_norm_g):
    bsz, seq, d = x.shape
    depth = w_in.shape[0]
    bw = d // 2
    assert bw % (2 * LANES) == 0 and seq % (2 * ATT_BLOCK) == 0 and seq % (C_GROUP * C_CHUNK) == 0
    assert seq % (B_GROUP * B_CHUNK) == 0
    assert w_in.shape[2] == 13 * bw + 4 * B_LORA
    x2 = x.reshape(bsz * seq, d)
    lb = jax.nn.softmax(hgrn_lb_logits.astype(F32), axis=1)
    lb = jnp.cumsum(lb, axis=1) - lb[:, :1]
    band = _bias_band(rel_bias, ATT_BLOCK)
    for l in range(depth):
        proj = _proj(x2, pre_norm_g[l], w_in[l].astype(BF16))
        lam_p = jnp.stack([lambda_q1[l], lambda_k1[l], lambda_q2[l], lambda_k2[l]]).astype(F32)
        ya = _attn(proj, lam_p, band, subln_g[l], bsz, seq, bw, l)
        yb, yc = _recurrent(proj, (rwkv_shift_mu[l], rwkv_w0[l], rwkv_a0[l], rwkv_w_up[l], rwkv_a_up[l], rwkv_k_k[l],
                                   rwkv_k_a[l], rwkv_r_k[l], rwkv_lnx_g[l], rwkv_lnx_b[l]),
                            (lb[:, l], hgrn_norm_g[l]), bsz, seq, bw)
        x2 = _out(ya, yb, yc, w_out[l].astype(BF16), post_norm_g[l], x2)
    return x2.reshape(bsz, seq, d)
```

```python
import functools
import math

import numpy as np
import jax
import jax.numpy as jnp
from jax import lax
from jax.experimental import pallas as pl
from jax.experimental.pallas import tpu as pltpu

F32 = jnp.float32
BF16 = jnp.bfloat16

LANES = 128
BF16_ROWS = 16
VMEM_LIMIT = 52 * 1024 * 1024

LOG2E = math.log2(math.e)

NORM_EPS = 1e-6
A_QK = 64
A_SUBLN_EPS = 1e-5
NUM_BUCKETS = 32
MAX_DISTANCE = 128
ATT_BLOCK = 256
ATT_SUB = 4
B_HEAD = 64
B_LORA = 64
B_LNX_EPS = 64e-5
B_CHUNK = 64
B_GROUP = 8
B_CUM = 4
C_CHUNK = 128
C_GROUP = 4
LB_FLOOR = 1e-30
ROW_BLOCK = 256

_NT = (((1,), (1,)), ((), ()))
_TN = (((0,), (0,)), ((), ()))


def _dot(a, b):
    return jnp.dot(a, b, preferred_element_type=F32)


def _dot_nt(a, b):
    return lax.dot_general(a, b, _NT, preferred_element_type=F32)


def _split_dot(a, b, split, passes):
    rem = a if split == 0 else b
    acc = None
    for _ in range(passes):
        piece = rem.astype(BF16)
        d = _dot(piece, b) if split == 0 else _dot(a, piece)
        acc = d if acc is None else acc + d
        rem = rem - piece.astype(F32)
    return acc


def _sigmoid(x):
    return 1.0 / (1.0 + jnp.exp(-x))


def _lockstep(*gens):
    results = [None] * len(gens)
    live = list(range(len(gens)))
    while live:
        for k in list(live):
            try:
                next(gens[k])
            except StopIteration as stop:
                results[k] = stop.value
                live.remove(k)
        yield
    return results


def _interleave(*gens):
    stepper = _lockstep(*gens)
    while True:
        try:
            next(stepper)
        except StopIteration as stop:
            return stop.value


def _cparams(sem):
    return pltpu.CompilerParams(dimension_semantics=sem, vmem_limit_bytes=VMEM_LIMIT)


def _proj_body(x_ref, g_ref, w_ref, o_ref, u_s):
    @pl.when(pl.program_id(1) == 0)
    def _():
        xf = x_ref[...]
        ms = jnp.mean(xf * xf, axis=-1, keepdims=True)
        u_s[...] = (xf * lax.rsqrt(ms + NORM_EPS) * g_ref[...]).astype(BF16)

    o_ref[...] = _dot(u_s[...], w_ref[...])


def _proj(x2, g, w, tm=1024, tn=1024):
    t, d = x2.shape
    p = w.shape[1]
    return pl.pallas_call(
        _proj_body,
        grid=(t // tm, pl.cdiv(p, tn)),
        in_specs=[pl.BlockSpec((tm, d), lambda i, j: (i, 0)),
                  pl.BlockSpec((1, d), lambda i, j: (0, 0)),
                  pl.BlockSpec((d, tn), lambda i, j: (0, j))],
        out_specs=pl.BlockSpec((tm, tn), lambda i, j: (i, j)),
        out_shape=jax.ShapeDtypeStruct((t, p), F32),
        scratch_shapes=[pltpu.VMEM((tm, d), BF16)],
        compiler_params=_cparams(("parallel", "arbitrary")),
        name="proj_in",
    )(x2, g.reshape(1, d), w)


def _out_body(ya_ref, yb_ref, yc_ref, w_ref, g_ref, x_ref, o_ref):
    bw = ya_ref.shape[1]
    m = (_dot(ya_ref[...], w_ref[0:bw, :]) + _dot(yb_ref[...], w_ref[bw:2 * bw, :])
         + _dot(yc_ref[...], w_ref[2 * bw:3 * bw, :]))
    ms = jnp.mean(m * m, axis=-1, keepdims=True)
    o_ref[...] = x_ref[...] + m * lax.rsqrt(ms + NORM_EPS) * g_ref[...]


def _out(ya, yb, yc, w, g, x2, tm=256):
    t, d = x2.shape
    bw = ya.shape[1]
    row = lambda i: (i, 0)
    fixed = lambda i: (0, 0)
    return pl.pallas_call(
        _out_body,
        grid=(t // tm,),
        in_specs=[pl.BlockSpec((tm, bw), row), pl.BlockSpec((tm, bw), row), pl.BlockSpec((tm, bw), row),
                  pl.BlockSpec((3 * bw, d), fixed), pl.BlockSpec((1, d), fixed), pl.BlockSpec((tm, d), row)],
        out_specs=pl.BlockSpec((tm, d), row),
        out_shape=jax.ShapeDtypeStruct((t, d), F32),
        compiler_params=_cparams(("parallel",)),
        name="proj_out",
    )(ya, yb, yc, w, g.reshape(1, d), x2)


def _t5_bucket(rel):
    half = NUM_BUCKETS // 2
    max_exact = half // 2
    n = jnp.abs(rel)
    nf = jnp.maximum(n, max_exact).astype(F32)
    large = max_exact + (jnp.log(nf / max_exact) / math.log(MAX_DISTANCE / max_exact)
                         * (half - max_exact)).astype(jnp.int32)
    large = jnp.minimum(large, half - 1)
    return jnp.where(rel > 0, half, 0) + jnp.where(n < max_exact, n, large)


def _bias_band(rel_bias, blk):
    c = blk - jnp.arange(2 * blk, dtype=jnp.int32)
    far = jnp.full((2 * blk,), 2 * blk, jnp.int32)
    rel = jnp.stack([c, c + blk, c - blk, -far, far, far, far, far])
    return jnp.transpose(rel_bias.astype(F32)[_t5_bucket(rel)], (2, 0, 1))


def _attn_body(lam_ref, q_ref, k_ref, v_ref, g_ref, band_ref, sg_ref, o_ref, tile_s, kb_s, vt_s, s_s, *,
               lambda_init):
    b = pl.program_id(1)
    qi = pl.program_id(2)
    tq = tile_s.shape[1]
    nsub = q_ref.shape[0] // tq
    nk = k_ref.shape[0] // tq

    @pl.when((b == 0) & (qi == 0))
    def _():
        vec = band_ref[0] * LOG2E
        for d in range(3):
            w = jnp.broadcast_to(vec[d:d + 1, :], (tq, 2 * tq))
            tile_s[d] = pltpu.roll(w, tq, 1, stride=1, stride_axis=0)[:, :tq]
        for d in range(3, 5):
            tile_s[d] = jnp.broadcast_to(vec[d:d + 1, :tq], (tq, tq))

    @pl.when(qi == 0)
    def _():
        row = lax.broadcasted_iota(jnp.int32, (BF16_ROWS, k_ref.shape[0]), 0)
        vt_s[LANES:LANES + BF16_ROWS, :] = jnp.where(row == 0, 1.0, 0.0).astype(BF16)
        for kj in range(nk):
            rows = pl.ds(kj * tq, tq)
            kb_s[rows, :] = k_ref[rows, :].astype(BF16)
            vt_s[0:LANES, kj * tq:(kj + 1) * tq] = v_ref[rows, :].T.astype(BF16)

    lp = lam_ref[...]
    lam = (jnp.exp(jnp.sum(lp[0:1] * lp[1:2], axis=-1, keepdims=True))
           - jnp.exp(jnp.sum(lp[2:3] * lp[3:4], axis=-1, keepdims=True)) + lambda_init)

    lane = lax.broadcasted_iota(jnp.int32, (tq, LANES), 1)

    def pass1(sub):
        q = q_ref[pl.ds(sub * tq, tq), :] * (A_QK ** -0.5 * LOG2E)
        q2 = jnp.concatenate([jnp.where(lane < A_QK, q, 0.0), jnp.where(lane >= A_QK, q, 0.0)], axis=0).astype(BF16)
        mx = None
        for kj in range(nk):
            d = kj - (qi * nsub + sub)
            bias = tile_s[jnp.where(d == 0, 0, jnp.where(d == 1, 1, jnp.where(d == -1, 2, jnp.where(d > 0, 4, 3))))]
            s = _dot_nt(kb_s[pl.ds(kj * tq, tq), :], q2) + jnp.concatenate([bias, bias], axis=1)
            s_s[sub % 2, kj] = s
            blk_max = jnp.max(s.reshape(tq // 8, 8, 2 * tq), axis=0)
            mx = blk_max if mx is None else jnp.maximum(mx, blk_max)
            yield
        return jnp.max(mx, axis=0, keepdims=True)

    def pass2(sub, m):
        acc = None
        for kj in range(nk):
            p = jnp.exp2(s_s[sub % 2, kj] - m).astype(BF16)
            pv = _dot(vt_s[:, kj * tq:(kj + 1) * tq], p)
            acc = pv if acc is None else acc + pv
            yield
        out = acc[0:LANES] / acc[LANES:LANES + 1]
        att = (out[:, :tq] - lam * out[:, tq:]).T
        ms = jnp.mean(att * att, axis=-1, keepdims=True)
        y = att * lax.rsqrt(ms + A_SUBLN_EPS) * sg_ref[...] * (1.0 - lambda_init)
        g = g_ref[pl.ds(sub * tq, tq), :]
        o_ref[pl.ds(sub * tq, tq), :] = (y * (g * _sigmoid(g))).astype(o_ref.dtype)

    m = _interleave(pass1(0))[0]
    for sub in range(nsub):
        if sub + 1 < nsub:
            _, m = _interleave(pass2(sub, m), pass1(sub + 1))
        else:
            _interleave(pass2(sub, m))


def _attn(proj, lam_p, band, subln_g, bsz, seq, bw, layer_idx):
    t = proj.shape[0]
    nb = bw // LANES
    heads = bw // (2 * A_QK)
    tq = ATT_BLOCK
    tstep = ATT_SUB * tq
    nq = seq // tstep
    lambda_init = 0.8 - 0.6 * math.exp(-0.3 * layer_idx)
    return pl.pallas_call(
        functools.partial(_attn_body, lambda_init=lambda_init),
        grid=(heads, bsz, nq),
        in_specs=[pl.BlockSpec((4, A_QK), lambda h, b, qi: (0, 0)),
                  pl.BlockSpec((tstep, LANES), lambda h, b, qi: (b * nq + qi, h)),
                  pl.BlockSpec((seq, LANES), lambda h, b, qi: (b, nb + h)),
                  pl.BlockSpec((seq, LANES), lambda h, b, qi: (b, 2 * nb + h)),
                  pl.BlockSpec((tstep, LANES), lambda h, b, qi: (b * nq + qi, 3 * nb + h)),
                  pl.BlockSpec((1, 8, 2 * tq), lambda h, b, qi: (h, 0, 0)),
                  pl.BlockSpec((1, LANES), lambda h, b, qi: (0, 0))],
        out_specs=pl.BlockSpec((tstep, LANES), lambda h, b, qi: (b * nq + qi, h)),
        out_shape=jax.ShapeDtypeStruct((t, bw), BF16),
        scratch_shapes=[pltpu.VMEM((5, tq, tq), F32), pltpu.VMEM((seq, LANES), BF16),
                        pltpu.VMEM((LANES + BF16_ROWS, seq), BF16),
                        pltpu.VMEM((2, seq // tq, tq, 2 * tq), F32)],
        compiler_params=_cparams(("arbitrary", "arbitrary", "arbitrary")),
        name="diff_attn",
    )(lam_p, proj, proj, proj, proj, band, subln_g.reshape(1, LANES))


def _hgrn_consts(c):
    i = np.arange(c)
    lo = (i[:, None] >= i[None, :]).astype(np.float32)
    levels = int(math.log2(c))
    pmask = np.zeros((2, levels + 1, c, c), np.float32)
    pmask[:, 0] = np.eye(c)
    for l in range(levels):
        m = 1 << l
        same = (i[:, None] // (2 * m)) == (i[None, :] // (2 * m))
        second = (i % (2 * m)) >= m
        pmask[0, l + 1] = same & second[:, None] & ~second[None, :]
        pmask[1, l + 1] = same & ~second[:, None] & second[None, :]
    return (jnp.asarray(lo, BF16), jnp.asarray(lo.T, BF16), jnp.asarray(pmask))


def _boundary(cum, m, rev):
    c = cum.shape[0]
    if 2 * m >= 8:
        n = c // (2 * m)
        x = cum.reshape(n, 2 * m, LANES)
        r = m if rev else m - 1
        return jnp.broadcast_to(x[:, r:r + 1, :], x.shape).reshape(c, LANES)
    x = cum.reshape(c // 8, 8, LANES)
    sub = lax.broadcasted_iota(jnp.int32, x.shape, 1)
    out = None
    for node in range(8 // (2 * m)):
        r = node * 2 * m + (m if rev else m - 1)
        bc = jnp.broadcast_to(x[:, r:r + 1, :], x.shape)
        out = bc if out is None else jnp.where(sub >= node * 2 * m, bc, out)
    return out.reshape(c, LANES)


def _hgrn_prep(z, q, v, lbp, tri, pmask_ref, rev):
    lb_floor, one_m_lb, lb_pos = lbp
    c = z.shape[0]
    d = 1 if rev else 0
    e = jnp.exp(-jnp.abs(z))
    r = 1.0 / (1.0 + e)
    pos = z >= 0.0
    sig = jnp.where(pos, r, e * r)
    log_sig = jnp.minimum(z, 0.0) - jnp.log(1.0 + e)
    lf = jnp.where(lb_pos, jnp.log(lb_floor + one_m_lb * sig), log_sig)
    k = one_m_lb * jnp.where(pos, e * r, r)
    cum = _split_dot(tri, lf, 1, 3)
    tot = cum[0:1] if rev else cum[c - 1:c]
    vb = v.astype(BF16)
    scores = _dot_nt(q.astype(BF16), k.astype(BF16)) * pmask_ref[d, 0]
    for l in range(int(math.log2(c))):
        e = jnp.exp(-jnp.abs(cum - _boundary(cum, 1 << l, rev)))
        scores = scores + _dot_nt((q * e).astype(BF16), (k * e).astype(BF16)) * pmask_ref[d, l + 1]
        if l == 3:
            yield
    o = _dot(scores.astype(BF16), vb)
    qd = (q * jnp.exp(cum)).astype(BF16)
    kv_t = _dot(v.T.astype(BF16), (k * jnp.exp(tot - cum)).astype(BF16))
    return o, qd, kv_t, jnp.exp(tot)


N_HGRN_IN = 10


def _hgrn_parts(q_ref, v_ref, zf_ref, zb_ref, g_ref, lb_ref, ng_ref, lo_ref, up_ref, pmask_ref,
                o_ref, acc_s):
    seq = q_ref.shape[0]
    c = lo_ref.shape[0]
    n = seq // c
    group = C_GROUP
    lbs = lb_ref[...]

    def lb_params(d):
        lb = lbs[d:d + 1]
        return jnp.maximum(lb, LB_FLOOR), 1.0 - lb, lb > 0.0

    lbp = (lb_params(0), lb_params(1))
    tri = (lo_ref[...], up_ref[...])
    z_ref = (zf_ref, zb_ref)
    acc_s[...] = jnp.zeros_like(acc_s)

    def body(i, carry):
        st = list(carry)
        base = (pl.multiple_of(i * (group * c), group * c),
                pl.multiple_of((n // group - 1 - i) * (group * c), group * c))
        prep = []
        for d in range(2):
            for j in range(group):
                rows = pl.ds(base[d] + j * c, c)
                prep.append((yield from _hgrn_prep(z_ref[d][rows, :], q_ref[rows, :], v_ref[rows, :], lbp[d], tri[d],
                                                   pmask_ref, d == 1)))
                yield
        for step in range(group):
            for d in range(2):
                j = step if d == 0 else group - 1 - step
                o, qd, kv_t, dec = prep[d * group + j]
                rows = pl.ds(base[d] + j * c, c)
                acc_s[rows, :] += o + _dot_nt(qd, st[d].astype(BF16))
                st[d] = st[d] * dec + kv_t
            yield
        return tuple(st)

    def finish(_):
        for r0 in range(0, seq, ROW_BLOCK):
            rows = pl.ds(r0, ROW_BLOCK)
            o = acc_s[rows, :]
            ms = jnp.mean(o * o, axis=-1, keepdims=True)
            g = g_ref[rows, :]
            o_ref[rows, :] = (o * lax.rsqrt(ms + NORM_EPS) * ng_ref[...] * (g * _sigmoid(g))).astype(o_ref.dtype)

    zero = jnp.zeros((LANES, LANES), F32)
    return n // group, (zero, zero), body, finish


def _hgrn_operands(proj, lb_l, norm_g, seq, bw):
    nb = bw // LANES
    c0 = 8 * nb + 2
    consts = _hgrn_consts(C_CHUNK)
    col = lambda off: pl.BlockSpec((seq, LANES), lambda b, h: (b, off + h))
    fixed2 = lambda shape: pl.BlockSpec(shape, lambda b, h: (0, 0))
    in_specs = [col(c0), col(c0 + nb), col(c0 + 2 * nb), col(c0 + 3 * nb), col(c0 + 4 * nb),
                pl.BlockSpec((2, LANES), lambda b, h: (0, h)), fixed2((1, LANES)),
                fixed2(consts[0].shape), fixed2(consts[1].shape),
                pl.BlockSpec(consts[2].shape, lambda b, h: (0, 0, 0, 0))]
    operands = (proj, proj, proj, proj, proj, lb_l, norm_g.reshape(1, LANES), *consts)
    assert len(operands) == N_HGRN_IN
    return in_specs, operands, [pltpu.VMEM((seq, LANES), F32)]


def _rwkv_consts(c, group):
    i = np.arange(c)
    j = np.arange(group * c)
    lo = ((j[:, None] >= j[None, :]) & (j[:, None] // c == j[None, :] // c)).astype(np.float32)
    tmask = np.stack([i[:, None] > i[None, :], i[:, None] >= i[None, :],
                      i[:, None] < i[None, :], i[:, None] <= i[None, :]]).astype(np.float32)
    levels = int(math.log2(c))
    lvl = np.zeros((levels, c, c), np.float32)
    for l in range(levels):
        m = 1 << l
        lvl[l] = ((i[:, None] // (2 * m)) == (i[None, :] // (2 * m))) & ((i[:, None] // m) != (i[None, :] // m))
    head = (np.arange(LANES)[:, None] // B_HEAD) == (np.arange(LANES)[None, :] // B_HEAD)
    tile4 = lambda a: np.concatenate([a] * 4, axis=-1)
    quarter = np.stack([np.broadcast_to((np.arange(4 * c) // c) == q, (c, 4 * c)) for q in range(4)])
    return (jnp.asarray(lo, BF16), jnp.asarray(lo.T, BF16), jnp.asarray(tile4(tmask)), jnp.asarray(tile4(lvl)),
            jnp.asarray(head.astype(np.float32), BF16), jnp.asarray(tile4(np.eye(c, dtype=np.float32))),
            jnp.asarray(quarter.astype(np.float32), BF16))


def _shifted(ref, mu, t0, rows, seq):
    x = ref[pl.ds(t0, rows), :]
    row = lax.broadcasted_iota(jnp.int32, x.shape, 0)
    if t0 == 0:
        prev = jnp.where(row == 0, 0.0, pltpu.roll(x, 1, 0))
    else:
        prev = ref[pl.ds(t0 - 1, rows), :]
    if t0 + rows == seq:
        nxt = jnp.where(row == rows - 1, 0.0, pltpu.roll(x, rows - 1, 0))
    else:
        nxt = ref[pl.ds(t0 + 1, rows), :]
    return x + mu[0:1] * (prev - x) + mu[1:2] * (nxt - x)


def _stack_heads(x, lane_lo):
    return jnp.concatenate([jnp.where(lane_lo, x, 0.0), jnp.where(lane_lo, 0.0, x)], axis=0)


def _rwkv_prep(r, lw, kk, kka, v, km, tri, tmask_ref, lvl_ref, eye_ref, quarter_ref, hmask, rev, group):
    c = r.shape[0] // group
    slab = tri.shape[0]
    lane_lo = lax.broadcasted_iota(jnp.int32, (c, LANES), 1) < B_HEAD
    row_lo = lax.broadcasted_iota(jnp.int32, (LANES, c), 0) < B_HEAD
    cw_all = jnp.concatenate([_split_dot(tri, lw[i:i + slab], 1, 3) for i in range(0, group * c, slab)], axis=0)
    strict = tmask_ref[2 if rev else 0]
    incl = tmask_ref[3 if rev else 1]
    eye = eye_ref[...]
    rows = [slice(i * c, (i + 1) * c) for i in range(group)]
    cw = [cw_all[s] for s in rows]
    tot = [x[0:1] if rev else x[c - 1:c] for x in cw]
    e_neg = [jnp.exp(-x) for x in cw]
    e_tot = [jnp.exp(t - x) for t, x in zip(tot, cw)]
    heads = lambda x: _stack_heads(x, lane_lo)
    bd = [kk[s] * jnp.exp(x - lw[s]) for s, x in zip(rows, cw)]
    ad = [-(kka[s] * e) for s, e in zip(rows, e_neg)]
    kd = [km[s] * e for s, e in zip(rows, e_neg)]
    rd = [r[s] * jnp.exp(x) for s, x in zip(rows, cw)]
    ae_t = [(-(kka[s] * e)).T for s, e in zip(rows, e_tot)]
    ke_t = [(km[s] * e).T.astype(BF16) for s, e in zip(rows, e_tot)]
    v2 = [heads(v[s]).astype(BF16) for s in rows]
    dcol = [jnp.broadcast_to(jnp.exp(t), (LANES, LANES)).T for t in tot]
    kv = [_dot(k, v[s].astype(BF16)) * hmask for k, s in zip(ke_t, rows)]
    yield
    pairs = [(2 * i, 2 * i + 1) for i in range(group // 2)]
    side = lambda xs: [jnp.concatenate([xs[a], xs[b]], axis=1) for a, b in pairs]
    lhs = side([jnp.concatenate([b, q], axis=0).astype(BF16) for b, q in zip(bd, rd)])
    rhs = [jnp.concatenate([heads(ad[ch]), heads(kd[ch])], axis=0).astype(BF16) for ch in range(group)]
    gram = [_dot_nt(x, _block_diag(rhs[a], rhs[b])) for x, (a, b) in zip(lhs, pairs)]
    pick = lambda g, i, j: jnp.concatenate([g[i * c:(i + 1) * c, j * LANES:(j + 1) * LANES],
                                            g[i * c:(i + 1) * c, (2 + j) * LANES:(3 + j) * LANES]], axis=1)
    ba = [pick(g, 0, 0) * strict for g in gram]
    bk = [pick(g, 0, 1) * strict for g in gram]
    ra = [pick(g, 1, 0) * incl for g in gram]
    rk = [pick(g, 1, 1) * incl for g in gram]
    yield
    spread = lambda x: jnp.concatenate([x * quarter_ref[q] for q in range(4)], axis=0)
    d = [eye + x * lvl_ref[0] for x in ba]
    for l in range(1, int(math.log2(c))):
        lv = lvl_ref[l]
        db = [x.astype(BF16) for x in d]
        dm = [_dot(x, spread((a * lv).astype(BF16))).astype(BF16) for x, a in zip(db, ba)]
        d = [x + _dot(y, spread(z)) for x, y, z in zip(d, dm, db)]
        yield
    xv = [_dot(jnp.concatenate([x, y], axis=0).astype(BF16), _block_diag(v2[a], v2[b]))
          for x, y, (a, b) in zip(bk, rk, pairs)]
    yield
    zrows = lambda y, ch: heads2_wide(jnp.concatenate([y, bd[ch]], axis=1)).astype(BF16)
    lane_lo2 = jnp.concatenate([lane_lo, lane_lo], axis=1)
    heads2_wide = lambda z: jnp.concatenate([jnp.where(lane_lo2, z, 0.0), jnp.where(lane_lo2, 0.0, z)], axis=0)
    tz = [_dot(x.astype(BF16), _block_diag(zrows(y[:c, :LANES], a), zrows(y[:c, LANES:], b)))
          for x, y, (a, b) in zip(d, xv, pairs)]
    tr, rae, add = [], [], []
    for i, (a, b) in enumerate(pairs):
        for h, ch in enumerate((a, b)):
            t = tz[i][:, 2 * h * LANES:(2 * h + 2) * LANES]
            y = xv[i][:, h * LANES:(h + 1) * LANES]
            aet = jnp.concatenate([jnp.where(row_lo, ae_t[ch], 0.0), jnp.where(row_lo, 0.0, ae_t[ch])], axis=1)
            tr.append(jnp.concatenate([t[:, LANES:], rd[ch]], axis=0).astype(BF16))
            rae.append(jnp.concatenate([ra[i][:, h * LANES:(h + 1) * LANES], aet], axis=0).astype(BF16))
            add.append(jnp.concatenate([t[:, :LANES], y[c:], kv[ch], dcol[ch]], axis=0))
    return tr, rae, add


def _block_diag(x, y):
    z = jnp.zeros_like(x)
    return jnp.concatenate([jnp.concatenate([x, z], axis=1), jnp.concatenate([z, y], axis=1)], axis=0)


def _rwkv_step(tr, rae, add, h):
    c = tr[0].shape[0] // 2
    lane_lo = lax.broadcasted_iota(jnp.int32, (c, LANES), 1) < B_HEAD
    side = lambda f, b: jnp.concatenate([f, b], axis=1)
    lanes = lambda x, d: x[:, d * LANES:(d + 1) * LANES]
    hb = h.astype(BF16)
    p = _dot(side(*tr), _block_diag(hb[:, :LANES], hb[:, LANES:]))
    u = [_stack_heads(add[d][0:c] + lanes(p[:c], d), lane_lo).astype(BF16) for d in range(2)]
    qq = _dot(side(*rae), _block_diag(*u))
    ys = [lanes(p[c:], d) + lanes(qq[:c], d) + add[d][c:2 * c] for d in range(2)]
    kv, dec = slice(2 * c, 2 * c + LANES), slice(2 * c + LANES, 2 * c + 2 * LANES)
    h = side(add[0][dec], add[1][dec]) * h + qq[c:] + side(add[0][kv], add[1][kv])
    return ys, h


N_RWKV_IN = 27
N_RWKV_SCRATCH = 11


def _rwkv_parts(r_ref, k_ref, v_ref, wd_ref, ad_ref, g_ref, mur_ref, muk_ref, muv_ref, muw_ref, mua_ref,
                w0_ref, a0_ref, wup_ref, aup_ref, kk_ref, ka_ref, rk_ref, lng_ref, lnb_ref,
                lo_ref, up_ref, tmask_ref, lvl_ref, head_ref, eye_ref, quarter_ref,
                o_ref, r_s, v_s, kk_s, lw_s, kka_s, km_s, bv_s, y_s, tr_s, rae_s, add_s):
    seq = r_ref.shape[0]
    c = B_CHUNK
    n = seq // c
    rb = ROW_BLOCK
    head = head_ref[...]
    hmask = head.astype(F32)
    lane_lo = lax.broadcasted_iota(jnp.int32, (rb, LANES), 1) < B_LORA
    for t0 in range(0, seq, rb):
        rows = pl.ds(t0, rb)
        r = _shifted(r_ref, mur_ref[...], t0, rb, seq)
        k = _shifted(k_ref, muk_ref[...], t0, rb, seq)
        v = _shifted(v_ref, muv_ref[...], t0, rb, seq)
        twd = jnp.tanh(_shifted(wd_ref, muw_ref[...], t0, rb, seq))
        ad = _shifted(ad_ref, mua_ref[...], t0, rb, seq)
        kk = k * kk_ref[...]
        kk = kk / jnp.maximum(jnp.sqrt(_split_dot(kk * kk, head, 0, 2)), 1e-12)
        kmsum = None
        for d in range(2):
            sel = lane_lo if d == 0 else jnp.logical_not(lane_lo)
            wl = _dot(jnp.where(sel, twd, 0.0).astype(BF16), wup_ref[...].astype(BF16))
            al = _dot(jnp.where(sel, ad, 0.0).astype(BF16), aup_ref[...].astype(BF16))
            lw_s[d, rows, :] = -math.exp(-0.5) * _sigmoid(w0_ref[d:d + 1, :] + wl)
            a = _sigmoid(a0_ref[d:d + 1, :] + al)
            km = k * (1.0 + (a - 1.0) * ka_ref[...])
            kka_s[d, rows, :] = kk * a
            km_s[d, rows, :] = km
            kmsum = km if kmsum is None else kmsum + km
        bonus = _split_dot(r * kmsum * rk_ref[...], head, 0, 2)
        r_s[rows, :] = r
        v_s[rows, :] = v
        kk_s[rows, :] = kk
        bv_s[rows, :] = bonus * v
    y_s[...] = jnp.zeros_like(y_s)
    tri_lo, tri_up = lo_ref[...], up_ref[...]
    group = B_GROUP
    gc = group * c

    n_it = n // group

    def bases(i):
        return pl.multiple_of(i * gc, gc), pl.multiple_of((n_it - 1 - i) * gc, gc)

    def prep(d, base, tri):
        rows = pl.ds(base, gc)
        tr, rae, add = yield from _rwkv_prep(
            r_s[rows, :], lw_s[d, rows, :], kk_s[rows, :], kka_s[d, rows, :], v_s[rows, :], km_s[d, rows, :],
            tri, tmask_ref, lvl_ref, eye_ref, quarter_ref, hmask, d == 1, group)
        for j in range(group):
            tr_s[d, j] = tr[j]
            rae_s[d, j] = rae[j]
            add_s[d, j] = add[j]

    def body(i, h):
        base_f, base_b = bases(i)
        yield from _lockstep(prep(0, base_f, tri_lo), prep(1, base_b, tri_up))
        for j in range(group):
            jb = group - 1 - j
            (yf, yb), h = _rwkv_step((tr_s[0, j], tr_s[1, jb]), (rae_s[0, j], rae_s[1, jb]),
                                     (add_s[0, j], add_s[1, jb]), h)
            y_s[pl.ds(base_f + j * c, c), :] += yf
            y_s[pl.ds(base_b + jb * c, c), :] += yb
            yield
        return h

    def finish(_):
        inv_n = 1.0 / B_HEAD
        for t0 in range(0, seq, rb):
            rows = pl.ds(t0, rb)
            y = y_s[rows, :]
            mu = _split_dot(y, head, 0, 2) * inv_n
            yc = y - mu
            var = _split_dot(yc * yc, head, 0, 2) * inv_n
            yn = yc * lax.rsqrt(var + B_LNX_EPS) * lng_ref[...] + lnb_ref[...]
            g = g_ref[rows, :]
            o_ref[rows, :] = ((yn + bv_s[rows, :]) * (g * _sigmoid(g))).astype(o_ref.dtype)

    return n_it, jnp.zeros((LANES, 2 * LANES), F32), body, finish


def _rwkv_operands(proj, mu, w0, a0, wup, aup, k_k, k_a, r_k, lnx_g, lnx_b, seq, bw):
    nb = bw // LANES
    c0 = 4 * nb
    consts = _rwkv_consts(B_CHUNK, B_CUM)
    col = lambda off: pl.BlockSpec((seq, LANES), lambda b, g: (b, off + g))
    colf = lambda off: pl.BlockSpec((seq, LANES), lambda b, g: (b, off))
    par2 = lambda off: pl.BlockSpec((2, LANES), lambda b, g: (0, off + g))
    par2f = lambda off: pl.BlockSpec((2, LANES), lambda b, g: (0, off))
    par1 = pl.BlockSpec((1, LANES), lambda b, g: (0, g))
    up = pl.BlockSpec((2 * B_LORA, LANES), lambda b, g: (0, g))

    def fixed(a):
        return pl.BlockSpec(a.shape, lambda b, g: (0,) * a.ndim)

    row = lambda a: a.reshape(1, bw)
    in_specs = [col(c0), col(c0 + nb), col(c0 + 2 * nb), colf(c0 + 3 * nb), colf(c0 + 3 * nb + 1), col(c0 + 3 * nb + 2),
                par2(0), par2(nb), par2(2 * nb), par2f(3 * nb), par2f(3 * nb + 1),
                par2(0), par2(0), up, up, par1, par1, par1, par1, par1] + [fixed(a) for a in consts]
    operands = (proj, proj, proj, proj, proj, proj, mu, mu, mu, mu, mu, w0, a0,
                wup.reshape(2 * B_LORA, bw), aup.reshape(2 * B_LORA, bw),
                row(k_k), row(k_a), row(r_k), row(lnx_g), row(lnx_b), *consts)
    scratch = [pltpu.VMEM((seq, LANES), F32), pltpu.VMEM((seq, LANES), F32), pltpu.VMEM((seq, LANES), F32),
               pltpu.VMEM((2, seq, LANES), F32), pltpu.VMEM((2, seq, LANES), F32),
               pltpu.VMEM((2, seq, LANES), F32), pltpu.VMEM((seq, LANES), F32),
               pltpu.VMEM((seq, LANES), F32),
               pltpu.VMEM((2, B_GROUP, 2 * B_CHUNK, LANES), BF16),
               pltpu.VMEM((2, B_GROUP, B_CHUNK + LANES, LANES), BF16),
               pltpu.VMEM((2, B_GROUP, 2 * B_CHUNK + 2 * LANES, LANES), F32)]
    assert len(operands) == N_RWKV_IN and len(scratch) == N_RWKV_SCRATCH
    return in_specs, operands, scratch


def _recurrent_body(*refs):
    rwkv_in = refs[:N_RWKV_IN]
    hgrn_in = refs[N_RWKV_IN:N_RWKV_IN + N_HGRN_IN]
    rwkv_out, hgrn_out = refs[N_RWKV_IN + N_HGRN_IN:N_RWKV_IN + N_HGRN_IN + 2]
    scratch = refs[N_RWKV_IN + N_HGRN_IN + 2:]
    trips_b, carry_b, body_b, finish_b = _rwkv_parts(*rwkv_in, rwkv_out, *scratch[:N_RWKV_SCRATCH])
    trips_c, carry_c, body_c, finish_c = _hgrn_parts(*hgrn_in, hgrn_out, *scratch[N_RWKV_SCRATCH:])
    assert trips_b == trips_c

    def body(i, carry):
        return tuple(_interleave(body_b(i, carry[0]), body_c(i, carry[1])))

    carry_b, carry_c = lax.fori_loop(0, trips_b, body, (carry_b, carry_c))
    finish_b(carry_b)
    finish_c(carry_c)


def _recurrent(proj, rwkv_params, hgrn_params, bsz, seq, bw):
    t = proj.shape[0]
    nb = bw // LANES
    specs_b, ops_b, scratch_b = _rwkv_operands(proj, *rwkv_params, seq, bw)
    specs_c, ops_c, scratch_c = _hgrn_operands(proj, *hgrn_params, seq, bw)
    out_spec = pl.BlockSpec((seq, LANES), lambda b, g: (b, g))
    out_shape = jax.ShapeDtypeStruct((t, bw), BF16)
    return pl.pallas_call(
        _recurrent_body,
        grid=(bsz, nb),
        in_specs=specs_b + specs_c,
        out_specs=[out_spec, out_spec],
        out_shape=[out_shape, out_shape],
        scratch_shapes=scratch_b + scratch_c,
        compiler_params=_cparams(("parallel", "parallel")),
        name="rwkv7_hgrn2",
    )(*ops_b, *ops_c)


def kernel(x, rel_bias, pre_norm_g, post_norm_g, w_in, w_out, lambda_q1, lambda_k1, lambda_q2, lambda_k2, subln_g, rwkv_shift_mu, rwkv_w0, rwkv_w_up, rwkv_a0, rwkv_a_up, rwkv_k_k, rwkv_k_a, rwkv_r_k, rwkv_lnx_g, rwkv_lnx_b, hgrn_lb_logits, hgrn_norm_g):
    bsz, seq, d = x.shape
    depth = w_in.shape[0]
    bw = d // 2
    assert bw % (2 * LANES) == 0 and seq % (ATT_SUB * ATT_BLOCK) == 0 and seq % (C_GROUP * C_CHUNK) == 0
    assert seq % (B_GROUP * B_CHUNK) == 0
    assert w_in.shape[2] == 13 * bw + 4 * B_LORA
    x2 = x.reshape(bsz * seq, d)
    lb = jax.nn.softmax(hgrn_lb_logits.astype(F32), axis=1)
    lb = jnp.cumsum(lb, axis=1) - lb[:, :1]
    band = _bias_band(rel_bias, ATT_BLOCK)
    for l in range(depth):
        proj = _proj(x2, pre_norm_g[l], w_in[l].astype(BF16))
        lam_p = jnp.stack([lambda_q1[l], lambda_k1[l], lambda_q2[l], lambda_k2[l]]).astype(F32)
        ya = _attn(proj, lam_p, band, subln_g[l], bsz, seq, bw, l)
        yb, yc = _recurrent(proj, (rwkv_shift_mu[l], rwkv_w0[l], rwkv_a0[l], rwkv_w_up[l], rwkv_a_up[l], rwkv_k_k[l],
                                   rwkv_k_a[l], rwkv_r_k[l], rwkv_lnx_g[l], rwkv_lnx_b[l]),
                            (lb[:, l], hgrn_norm_g[l]), bsz, seq, bw)
        x2 = _out(ya, yb, yc, w_out[l].astype(BF16), post_norm_g[l], x2)
    return x2.reshape(bsz, seq, d)
```

```python
import functools
import math

import numpy as np
import jax
import jax.numpy as jnp
from jax import lax
from jax.experimental import pallas as pl
from jax.experimental.pallas import tpu as pltpu

F32 = jnp.float32
BF16 = jnp.bfloat16

LANES = 128
BF16_ROWS = 16
VMEM_LIMIT = 52 * 1024 * 1024

LOG2E = math.log2(math.e)

NORM_EPS = 1e-6
A_QK = 64
A_SUBLN_EPS = 1e-5
NUM_BUCKETS = 32
MAX_DISTANCE = 128
ATT_BLOCK = 256
ATT_SUB = 8
B_HEAD = 64
B_LORA = 64
B_LNX_EPS = 64e-5
B_CHUNK = 64
B_GROUP = 8
B_CUM = 4
C_CHUNK = 128
C_GROUP = 4
LB_FLOOR = 1e-30
ROW_BLOCK = 256

_NT = (((1,), (1,)), ((), ()))
_TN = (((0,), (0,)), ((), ()))


def _dot(a, b):
    return jnp.dot(a, b, preferred_element_type=F32)


def _dot_nt(a, b):
    return lax.dot_general(a, b, _NT, preferred_element_type=F32)


def _split_dot(a, b, split, passes):
    rem = a if split == 0 else b
    acc = None
    for _ in range(passes):
        piece = rem.astype(BF16)
        d = _dot(piece, b) if split == 0 else _dot(a, piece)
        acc = d if acc is None else acc + d
        rem = rem - piece.astype(F32)
    return acc


def _sigmoid(x):
    return 1.0 / (1.0 + jnp.exp(-x))


def _lockstep(*gens):
    results = [None] * len(gens)
    live = list(range(len(gens)))
    while live:
        for k in list(live):
            try:
                next(gens[k])
            except StopIteration as stop:
                results[k] = stop.value
                live.remove(k)
        yield
    return results


def _interleave(*gens):
    stepper = _lockstep(*gens)
    while True:
        try:
            next(stepper)
        except StopIteration as stop:
            return stop.value


def _cparams(sem):
    return pltpu.CompilerParams(dimension_semantics=sem, vmem_limit_bytes=VMEM_LIMIT)


def _proj_body(x_ref, g_ref, w_ref, o_ref, u_s):
    @pl.when(pl.program_id(1) == 0)
    def _():
        xf = x_ref[...]
        ms = jnp.mean(xf * xf, axis=-1, keepdims=True)
        u_s[...] = (xf * lax.rsqrt(ms + NORM_EPS) * g_ref[...]).astype(BF16)

    o_ref[...] = _dot(u_s[...], w_ref[...])


def _proj(x2, g, w, layer, tm=1024, tn=1024):
    t, d = x2.shape
    p = w.shape[2]
    return pl.pallas_call(
        _proj_body,
        grid=(t // tm, pl.cdiv(p, tn)),
        in_specs=[pl.BlockSpec((tm, d), lambda i, j: (i, 0)),
                  pl.BlockSpec((1, d), lambda i, j: (0, 0)),
                  pl.BlockSpec((None, d, tn), lambda i, j: (layer, 0, j))],
        out_specs=pl.BlockSpec((tm, tn), lambda i, j: (i, j)),
        out_shape=jax.ShapeDtypeStruct((t, p), F32),
        scratch_shapes=[pltpu.VMEM((tm, d), BF16)],
        compiler_params=_cparams(("parallel", "arbitrary")),
        name="proj_in",
    )(x2, g.reshape(1, d), w)


def _out_body(ya_ref, yb_ref, yc_ref, w_ref, g_ref, x_ref, o_ref):
    bw = ya_ref.shape[1]
    m = (_dot(ya_ref[...], w_ref[0:bw, :]) + _dot(yb_ref[...], w_ref[bw:2 * bw, :])
         + _dot(yc_ref[...], w_ref[2 * bw:3 * bw, :]))
    ms = jnp.mean(m * m, axis=-1, keepdims=True)
    o_ref[...] = x_ref[...] + m * lax.rsqrt(ms + NORM_EPS) * g_ref[...]


def _out(ya, yb, yc, w, layer, g, x2, tm=256):
    t, d = x2.shape
    bw = ya.shape[1]
    row = lambda i: (i, 0)
    fixed = lambda i: (0, 0)
    return pl.pallas_call(
        _out_body,
        grid=(t // tm,),
        in_specs=[pl.BlockSpec((tm, bw), row), pl.BlockSpec((tm, bw), row), pl.BlockSpec((tm, bw), row),
                  pl.BlockSpec((None, 3 * bw, d), lambda i: (layer, 0, 0)), pl.BlockSpec((1, d), fixed),
                  pl.BlockSpec((tm, d), row)],
        out_specs=pl.BlockSpec((tm, d), row),
        out_shape=jax.ShapeDtypeStruct((t, d), F32),
        compiler_params=_cparams(("parallel",)),
        name="proj_out",
    )(ya, yb, yc, w, g.reshape(1, d), x2)


def _t5_bucket(rel):
    half = NUM_BUCKETS // 2
    max_exact = half // 2
    n = jnp.abs(rel)
    nf = jnp.maximum(n, max_exact).astype(F32)
    large = max_exact + (jnp.log(nf / max_exact) / math.log(MAX_DISTANCE / max_exact)
                         * (half - max_exact)).astype(jnp.int32)
    large = jnp.minimum(large, half - 1)
    return jnp.where(rel > 0, half, 0) + jnp.where(n < max_exact, n, large)


def _bias_band(rel_bias, blk):
    c = blk - jnp.arange(2 * blk, dtype=jnp.int32)
    far = jnp.full((2 * blk,), 2 * blk, jnp.int32)
    rel = jnp.stack([c, c + blk, c - blk, -far, far, far, far, far])
    return jnp.transpose(rel_bias.astype(F32)[_t5_bucket(rel)], (2, 0, 1))


def _attn_body(lam_ref, q_ref, k_ref, v_ref, g_ref, band_ref, sg_ref, o_ref, tile_s, kb_s, vt_s, s_s, *,
               lambda_init):
    b = pl.program_id(1)
    qi = pl.program_id(2)
    tq = tile_s.shape[1]
    nsub = q_ref.shape[0] // tq
    nk = k_ref.shape[0] // tq

    @pl.when((b == 0) & (qi == 0))
    def _():
        vec = band_ref[0] * LOG2E
        for d in range(3):
            w = jnp.broadcast_to(vec[d:d + 1, :], (tq, 2 * tq))
            tile_s[d] = pltpu.roll(w, tq, 1, stride=1, stride_axis=0)[:, :tq]
        for d in range(3, 5):
            tile_s[d] = jnp.broadcast_to(vec[d:d + 1, :tq], (tq, tq))

    @pl.when(qi == 0)
    def _():
        row = lax.broadcasted_iota(jnp.int32, (BF16_ROWS, k_ref.shape[0]), 0)
        vt_s[LANES:LANES + BF16_ROWS, :] = jnp.where(row == 0, 1.0, 0.0).astype(BF16)
        for kj in range(nk):
            rows = pl.ds(kj * tq, tq)
            kb_s[rows, :] = k_ref[rows, :].astype(BF16)
            vt_s[0:LANES, kj * tq:(kj + 1) * tq] = v_ref[rows, :].T.astype(BF16)

    lp = lam_ref[...]
    lam = (jnp.exp(jnp.sum(lp[0:1] * lp[1:2], axis=-1, keepdims=True))
           - jnp.exp(jnp.sum(lp[2:3] * lp[3:4], axis=-1, keepdims=True)) + lambda_init)

    lane = lax.broadcasted_iota(jnp.int32, (tq, LANES), 1)

    def pass1(sub):
        q = q_ref[pl.ds(sub * tq, tq), :] * (A_QK ** -0.5 * LOG2E)
        q2 = jnp.concatenate([jnp.where(lane < A_QK, q, 0.0), jnp.where(lane >= A_QK, q, 0.0)], axis=0).astype(BF16)
        mx = None
        for kj in range(nk):
            d = kj - (qi * nsub + sub)
            bias = tile_s[jnp.where(d == 0, 0, jnp.where(d == 1, 1, jnp.where(d == -1, 2, jnp.where(d > 0, 4, 3))))]
            s = _dot_nt(kb_s[pl.ds(kj * tq, tq), :], q2) + jnp.concatenate([bias, bias], axis=1)
            s_s[sub % 2, kj] = s
            blk_max = jnp.max(s.reshape(tq // 8, 8, 2 * tq), axis=0)
            mx = blk_max if mx is None else jnp.maximum(mx, blk_max)
            yield
        return jnp.max(mx, axis=0, keepdims=True)

    def pass2(sub, m):
        acc = None
        for kj in range(nk):
            p = jnp.exp2(s_s[sub % 2, kj] - m).astype(BF16)
            pv = _dot(vt_s[:, kj * tq:(kj + 1) * tq], p)
            acc = pv if acc is None else acc + pv
            yield
        out = acc[0:LANES] / acc[LANES:LANES + 1]
        att = (out[:, :tq] - lam * out[:, tq:]).T
        ms = jnp.mean(att * att, axis=-1, keepdims=True)
        y = att * lax.rsqrt(ms + A_SUBLN_EPS) * sg_ref[...] * (1.0 - lambda_init)
        g = g_ref[pl.ds(sub * tq, tq), :]
        o_ref[pl.ds(sub * tq, tq), :] = (y * (g * _sigmoid(g))).astype(o_ref.dtype)

    m = _interleave(pass1(0))[0]
    for sub in range(nsub):
        if sub + 1 < nsub:
            _, m = _interleave(pass2(sub, m), pass1(sub + 1))
        else:
            _interleave(pass2(sub, m))


def _attn(proj, lam_p, band, subln_g, bsz, seq, bw, layer_idx):
    t = proj.shape[0]
    nb = bw // LANES
    heads = bw // (2 * A_QK)
    tq = ATT_BLOCK
    tstep = ATT_SUB * tq
    nq = seq // tstep
    lambda_init = 0.8 - 0.6 * math.exp(-0.3 * layer_idx)
    return pl.pallas_call(
        functools.partial(_attn_body, lambda_init=lambda_init),
        grid=(heads, bsz, nq),
        in_specs=[pl.BlockSpec((4, A_QK), lambda h, b, qi: (0, 0)),
                  pl.BlockSpec((tstep, LANES), lambda h, b, qi: (b * nq + qi, h)),
                  pl.BlockSpec((seq, LANES), lambda h, b, qi: (b, nb + h)),
                  pl.BlockSpec((seq, LANES), lambda h, b, qi: (b, 2 * nb + h)),
                  pl.BlockSpec((tstep, LANES), lambda h, b, qi: (b * nq + qi, 3 * nb + h)),
                  pl.BlockSpec((1, 8, 2 * tq), lambda h, b, qi: (h, 0, 0)),
                  pl.BlockSpec((1, LANES), lambda h, b, qi: (0, 0))],
        out_specs=pl.BlockSpec((tstep, LANES), lambda h, b, qi: (b * nq + qi, h)),
        out_shape=jax.ShapeDtypeStruct((t, bw), BF16),
        scratch_shapes=[pltpu.VMEM((5, tq, tq), F32), pltpu.VMEM((seq, LANES), BF16),
                        pltpu.VMEM((LANES + BF16_ROWS, seq), BF16),
                        pltpu.VMEM((2, seq // tq, tq, 2 * tq), F32)],
        compiler_params=_cparams(("arbitrary", "arbitrary", "arbitrary")),
        name="diff_attn",
    )(lam_p, proj, proj, proj, proj, band, subln_g.reshape(1, LANES))


def _hgrn_consts(c):
    i = np.arange(c)
    lo = (i[:, None] >= i[None, :]).astype(np.float32)
    levels = int(math.log2(c))
    pmask = np.zeros((2, levels + 1, c, c), np.float32)
    pmask[:, 0] = np.eye(c)
    for l in range(levels):
        m = 1 << l
        same = (i[:, None] // (2 * m)) == (i[None, :] // (2 * m))
        second = (i % (2 * m)) >= m
        pmask[0, l + 1] = same & second[:, None] & ~second[None, :]
        pmask[1, l + 1] = same & ~second[:, None] & second[None, :]
    return (jnp.asarray(lo, BF16), jnp.asarray(lo.T, BF16), jnp.asarray(pmask))


def _boundary(cum, m, rev):
    c = cum.shape[0]
    if 2 * m >= 8:
        n = c // (2 * m)
        x = cum.reshape(n, 2 * m, LANES)
        r = m if rev else m - 1
        return jnp.broadcast_to(x[:, r:r + 1, :], x.shape).reshape(c, LANES)
    x = cum.reshape(c // 8, 8, LANES)
    sub = lax.broadcasted_iota(jnp.int32, x.shape, 1)
    out = None
    for node in range(8 // (2 * m)):
        r = node * 2 * m + (m if rev else m - 1)
        bc = jnp.broadcast_to(x[:, r:r + 1, :], x.shape)
        out = bc if out is None else jnp.where(sub >= node * 2 * m, bc, out)
    return out.reshape(c, LANES)


def _hgrn_prep(z, q, v, lbp, tri, pmask_ref, rev):
    lb_floor, one_m_lb, lb_pos = lbp
    c = z.shape[0]
    d = 1 if rev else 0
    e = jnp.exp(-jnp.abs(z))
    r = 1.0 / (1.0 + e)
    pos = z >= 0.0
    sig = jnp.where(pos, r, e * r)
    log_sig = jnp.minimum(z, 0.0) - jnp.log(1.0 + e)
    lf = jnp.where(lb_pos, jnp.log(lb_floor + one_m_lb * sig), log_sig)
    k = one_m_lb * jnp.where(pos, e * r, r)
    cum = _split_dot(tri, lf, 1, 3)
    tot = cum[0:1] if rev else cum[c - 1:c]
    vb = v.astype(BF16)
    scores = _dot_nt(q.astype(BF16), k.astype(BF16)) * pmask_ref[d, 0]
    for l in range(int(math.log2(c))):
        e = jnp.exp(-jnp.abs(cum - _boundary(cum, 1 << l, rev)))
        scores = scores + _dot_nt((q * e).astype(BF16), (k * e).astype(BF16)) * pmask_ref[d, l + 1]
        if l == 3:
            yield
    o = _dot(scores.astype(BF16), vb)
    qd = (q * jnp.exp(cum)).astype(BF16)
    kv_t = _dot(v.T.astype(BF16), (k * jnp.exp(tot - cum)).astype(BF16))
    return o, qd, kv_t, jnp.exp(tot)


N_HGRN_IN = 10


def _hgrn_parts(q_ref, v_ref, zf_ref, zb_ref, g_ref, lb_ref, ng_ref, lo_ref, up_ref, pmask_ref,
                o_ref, acc_s):
    seq = q_ref.shape[0]
    c = lo_ref.shape[0]
    n = seq // c
    group = C_GROUP
    lbs = lb_ref[...]

    def lb_params(d):
        lb = lbs[d:d + 1]
        return jnp.maximum(lb, LB_FLOOR), 1.0 - lb, lb > 0.0

    lbp = (lb_params(0), lb_params(1))
    tri = (lo_ref[...], up_ref[...])
    z_ref = (zf_ref, zb_ref)
    acc_s[...] = jnp.zeros_like(acc_s)

    def body(i, carry):
        st = list(carry)
        base = (pl.multiple_of(i * (group * c), group * c),
                pl.multiple_of((n // group - 1 - i) * (group * c), group * c))
        prep = []
        for d in range(2):
            for j in range(group):
                rows = pl.ds(base[d] + j * c, c)
                prep.append((yield from _hgrn_prep(z_ref[d][rows, :], q_ref[rows, :], v_ref[rows, :], lbp[d], tri[d],
                                                   pmask_ref, d == 1)))
                yield
        for step in range(group):
            for d in range(2):
                j = step if d == 0 else group - 1 - step
                o, qd, kv_t, dec = prep[d * group + j]
                rows = pl.ds(base[d] + j * c, c)
                acc_s[rows, :] += o + _dot_nt(qd, st[d].astype(BF16))
                st[d] = st[d] * dec + kv_t
            yield
        return tuple(st)

    def finish(_):
        for r0 in range(0, seq, ROW_BLOCK):
            rows = pl.ds(r0, ROW_BLOCK)
            o = acc_s[rows, :]
            ms = jnp.mean(o * o, axis=-1, keepdims=True)
            g = g_ref[rows, :]
            o_ref[rows, :] = (o * lax.rsqrt(ms + NORM_EPS) * ng_ref[...] * (g * _sigmoid(g))).astype(o_ref.dtype)

    zero = jnp.zeros((LANES, LANES), F32)
    return n // group, (zero, zero), body, finish


def _hgrn_operands(proj, lb_l, norm_g, seq, bw):
    nb = bw // LANES
    c0 = 8 * nb + 2
    consts = _hgrn_consts(C_CHUNK)
    col = lambda off: pl.BlockSpec((seq, LANES), lambda b, h: (b, off + h))
    fixed2 = lambda shape: pl.BlockSpec(shape, lambda b, h: (0, 0))
    in_specs = [col(c0), col(c0 + nb), col(c0 + 2 * nb), col(c0 + 3 * nb), col(c0 + 4 * nb),
                pl.BlockSpec((2, LANES), lambda b, h: (0, h)), fixed2((1, LANES)),
                fixed2(consts[0].shape), fixed2(consts[1].shape),
                pl.BlockSpec(consts[2].shape, lambda b, h: (0, 0, 0, 0))]
    operands = (proj, proj, proj, proj, proj, lb_l, norm_g.reshape(1, LANES), *consts)
    assert len(operands) == N_HGRN_IN
    return in_specs, operands, [pltpu.VMEM((seq, LANES), F32)]


def _rwkv_consts(c, group):
    i = np.arange(c)
    j = np.arange(group * c)
    lo = ((j[:, None] >= j[None, :]) & (j[:, None] // c == j[None, :] // c)).astype(np.float32)
    tmask = np.stack([i[:, None] > i[None, :], i[:, None] >= i[None, :],
                      i[:, None] < i[None, :], i[:, None] <= i[None, :]]).astype(np.float32)
    levels = int(math.log2(c))
    lvl = np.zeros((levels, c, c), np.float32)
    for l in range(levels):
        m = 1 << l
        lvl[l] = ((i[:, None] // (2 * m)) == (i[None, :] // (2 * m))) & ((i[:, None] // m) != (i[None, :] // m))
    head = (np.arange(LANES)[:, None] // B_HEAD) == (np.arange(LANES)[None, :] // B_HEAD)
    tile4 = lambda a: np.concatenate([a] * 4, axis=-1)
    quarter = np.stack([np.broadcast_to((np.arange(4 * c) // c) == q, (c, 4 * c)) for q in range(4)])
    return (jnp.asarray(lo, BF16), jnp.asarray(lo.T, BF16), jnp.asarray(tile4(tmask)), jnp.asarray(tile4(lvl)),
            jnp.asarray(head.astype(np.float32), BF16), jnp.asarray(tile4(np.eye(c, dtype=np.float32))),
            jnp.asarray(quarter.astype(np.float32), BF16))


def _shifted(ref, mu, t0, rows, seq):
    x = ref[pl.ds(t0, rows), :]
    row = lax.broadcasted_iota(jnp.int32, x.shape, 0)
    if t0 == 0:
        prev = jnp.where(row == 0, 0.0, pltpu.roll(x, 1, 0))
    else:
        prev = ref[pl.ds(t0 - 1, rows), :]
    if t0 + rows == seq:
        nxt = jnp.where(row == rows - 1, 0.0, pltpu.roll(x, rows - 1, 0))
    else:
        nxt = ref[pl.ds(t0 + 1, rows), :]
    return x + mu[0:1] * (prev - x) + mu[1:2] * (nxt - x)


def _stack_heads(x, lane_lo):
    return jnp.concatenate([jnp.where(lane_lo, x, 0.0), jnp.where(lane_lo, 0.0, x)], axis=0)


def _rwkv_prep(r, lw, kk, kka, v, km, tri, tmask_ref, lvl_ref, eye_ref, quarter_ref, hmask, rev, group):
    c = r.shape[0] // group
    slab = tri.shape[0]
    lane_lo = lax.broadcasted_iota(jnp.int32, (c, LANES), 1) < B_HEAD
    row_lo = lax.broadcasted_iota(jnp.int32, (LANES, c), 0) < B_HEAD
    cw_all = jnp.concatenate([_split_dot(tri, lw[i:i + slab], 1, 3) for i in range(0, group * c, slab)], axis=0)
    strict = tmask_ref[2 if rev else 0]
    incl = tmask_ref[3 if rev else 1]
    eye = eye_ref[...]
    rows = [slice(i * c, (i + 1) * c) for i in range(group)]
    cw = [cw_all[s] for s in rows]
    tot = [x[0:1] if rev else x[c - 1:c] for x in cw]
    e_neg = [jnp.exp(-x) for x in cw]
    e_tot = [jnp.exp(t - x) for t, x in zip(tot, cw)]
    heads = lambda x: _stack_heads(x, lane_lo)
    bd = [kk[s] * jnp.exp(x - lw[s]) for s, x in zip(rows, cw)]
    ad = [-(kka[s] * e) for s, e in zip(rows, e_neg)]
    kd = [km[s] * e for s, e in zip(rows, e_neg)]
    rd = [r[s] * jnp.exp(x) for s, x in zip(rows, cw)]
    ae_t = [(-(kka[s] * e)).T for s, e in zip(rows, e_tot)]
    ke_t = [(km[s] * e).T.astype(BF16) for s, e in zip(rows, e_tot)]
    v2 = [heads(v[s]).astype(BF16) for s in rows]
    dcol = [jnp.broadcast_to(jnp.exp(t), (LANES, LANES)).T for t in tot]
    kv = [_dot(k, v[s].astype(BF16)) * hmask for k, s in zip(ke_t, rows)]
    yield
    pairs = [(2 * i, 2 * i + 1) for i in range(group // 2)]
    side = lambda xs: [jnp.concatenate([xs[a], xs[b]], axis=1) for a, b in pairs]
    lhs = side([jnp.concatenate([b, q], axis=0).astype(BF16) for b, q in zip(bd, rd)])
    rhs = [jnp.concatenate([heads(ad[ch]), heads(kd[ch])], axis=0).astype(BF16) for ch in range(group)]
    gram = [_dot_nt(x, _block_diag(rhs[a], rhs[b])) for x, (a, b) in zip(lhs, pairs)]
    pick = lambda g, i, j: jnp.concatenate([g[i * c:(i + 1) * c, j * LANES:(j + 1) * LANES],
                                            g[i * c:(i + 1) * c, (2 + j) * LANES:(3 + j) * LANES]], axis=1)
    ba = [pick(g, 0, 0) * strict for g in gram]
    bk = [pick(g, 0, 1) * strict for g in gram]
    ra = [pick(g, 1, 0) * incl for g in gram]
    rk = [pick(g, 1, 1) * incl for g in gram]
    yield
    spread = lambda x: jnp.concatenate([x * quarter_ref[q] for q in range(4)], axis=0)
    d = [eye + x * lvl_ref[0] for x in ba]
    for l in range(1, int(math.log2(c))):
        lv = lvl_ref[l]
        db = [x.astype(BF16) for x in d]
        dm = [_dot(x, spread((a * lv).astype(BF16))).astype(BF16) for x, a in zip(db, ba)]
        d = [x + _dot(y, spread(z)) for x, y, z in zip(d, dm, db)]
        yield
    xv = [_dot(jnp.concatenate([x, y], axis=0).astype(BF16), _block_diag(v2[a], v2[b]))
          for x, y, (a, b) in zip(bk, rk, pairs)]
    yield
    zrows = lambda y, ch: heads2_wide(jnp.concatenate([y, bd[ch]], axis=1)).astype(BF16)
    lane_lo2 = jnp.concatenate([lane_lo, lane_lo], axis=1)
    heads2_wide = lambda z: jnp.concatenate([jnp.where(lane_lo2, z, 0.0), jnp.where(lane_lo2, 0.0, z)], axis=0)
    tz = [_dot(x.astype(BF16), _block_diag(zrows(y[:c, :LANES], a), zrows(y[:c, LANES:], b)))
          for x, y, (a, b) in zip(d, xv, pairs)]
    tr, rae, add = [], [], []
    for i, (a, b) in enumerate(pairs):
        for h, ch in enumerate((a, b)):
            t = tz[i][:, 2 * h * LANES:(2 * h + 2) * LANES]
            y = xv[i][:, h * LANES:(h + 1) * LANES]
            aet = jnp.concatenate([jnp.where(row_lo, ae_t[ch], 0.0), jnp.where(row_lo, 0.0, ae_t[ch])], axis=1)
            tr.append(jnp.concatenate([t[:, LANES:], rd[ch]], axis=0).astype(BF16))
            rae.append(jnp.concatenate([ra[i][:, h * LANES:(h + 1) * LANES], aet], axis=0).astype(BF16))
            add.append(jnp.concatenate([t[:, :LANES], y[c:], kv[ch], dcol[ch]], axis=0))
    return tr, rae, add


def _block_diag(x, y):
    z = jnp.zeros_like(x)
    return jnp.concatenate([jnp.concatenate([x, z], axis=1), jnp.concatenate([z, y], axis=1)], axis=0)


def _rwkv_step(tr, rae, add, h):
    c = tr[0].shape[0] // 2
    lane_lo = lax.broadcasted_iota(jnp.int32, (c, LANES), 1) < B_HEAD
    side = lambda f, b: jnp.concatenate([f, b], axis=1)
    lanes = lambda x, d: x[:, d * LANES:(d + 1) * LANES]
    hb = h.astype(BF16)
    p = _dot(side(*tr), _block_diag(hb[:, :LANES], hb[:, LANES:]))
    u = [_stack_heads(add[d][0:c] + lanes(p[:c], d), lane_lo).astype(BF16) for d in range(2)]
    qq = _dot(side(*rae), _block_diag(*u))
    ys = [lanes(p[c:], d) + lanes(qq[:c], d) + add[d][c:2 * c] for d in range(2)]
    kv, dec = slice(2 * c, 2 * c + LANES), slice(2 * c + LANES, 2 * c + 2 * LANES)
    h = side(add[0][dec], add[1][dec]) * h + qq[c:] + side(add[0][kv], add[1][kv])
    return ys, h


N_RWKV_IN = 27
N_RWKV_SCRATCH = 11


def _rwkv_parts(r_ref, k_ref, v_ref, wd_ref, ad_ref, g_ref, mur_ref, muk_ref, muv_ref, muw_ref, mua_ref,
                w0_ref, a0_ref, wup_ref, aup_ref, kk_ref, ka_ref, rk_ref, lng_ref, lnb_ref,
                lo_ref, up_ref, tmask_ref, lvl_ref, head_ref, eye_ref, quarter_ref,
                o_ref, r_s, v_s, kk_s, lw_s, kka_s, km_s, bv_s, y_s, tr_s, rae_s, add_s):
    seq = r_ref.shape[0]
    c = B_CHUNK
    n = seq // c
    rb = ROW_BLOCK
    head = head_ref[...]
    hmask = head.astype(F32)
    lane_lo = lax.broadcasted_iota(jnp.int32, (rb, LANES), 1) < B_LORA
    for t0 in range(0, seq, rb):
        rows = pl.ds(t0, rb)
        r = _shifted(r_ref, mur_ref[...], t0, rb, seq)
        k = _shifted(k_ref, muk_ref[...], t0, rb, seq)
        v = _shifted(v_ref, muv_ref[...], t0, rb, seq)
        twd = jnp.tanh(_shifted(wd_ref, muw_ref[...], t0, rb, seq))
        ad = _shifted(ad_ref, mua_ref[...], t0, rb, seq)
        kk = k * kk_ref[...]
        kk = kk / jnp.maximum(jnp.sqrt(_split_dot(kk * kk, head, 0, 2)), 1e-12)
        kmsum = None
        for d in range(2):
            sel = lane_lo if d == 0 else jnp.logical_not(lane_lo)
            wl = _dot(jnp.where(sel, twd, 0.0).astype(BF16), wup_ref[...].astype(BF16))
            al = _dot(jnp.where(sel, ad, 0.0).astype(BF16), aup_ref[...].astype(BF16))
            lw_s[d, rows, :] = -math.exp(-0.5) * _sigmoid(w0_ref[d:d + 1, :] + wl)
            a = _sigmoid(a0_ref[d:d + 1, :] + al)
            km = k * (1.0 + (a - 1.0) * ka_ref[...])
            kka_s[d, rows, :] = kk * a
            km_s[d, rows, :] = km
            kmsum = km if kmsum is None else kmsum + km
        bonus = _split_dot(r * kmsum * rk_ref[...], head, 0, 2)
        r_s[rows, :] = r
        v_s[rows, :] = v
        kk_s[rows, :] = kk
        bv_s[rows, :] = bonus * v
    y_s[...] = jnp.zeros_like(y_s)
    tri_lo, tri_up = lo_ref[...], up_ref[...]
    group = B_GROUP
    gc = group * c

    n_it = n // group

    def bases(i):
        return pl.multiple_of(i * gc, gc), pl.multiple_of((n_it - 1 - i) * gc, gc)

    def prep(d, base, tri):
        rows = pl.ds(base, gc)
        tr, rae, add = yield from _rwkv_prep(
            r_s[rows, :], lw_s[d, rows, :], kk_s[rows, :], kka_s[d, rows, :], v_s[rows, :], km_s[d, rows, :],
            tri, tmask_ref, lvl_ref, eye_ref, quarter_ref, hmask, d == 1, group)
        for j in range(group):
            tr_s[d, j] = tr[j]
            rae_s[d, j] = rae[j]
            add_s[d, j] = add[j]

    def body(i, h):
        base_f, base_b = bases(i)
        yield from _lockstep(prep(0, base_f, tri_lo), prep(1, base_b, tri_up))
        for j in range(group):
            jb = group - 1 - j
            (yf, yb), h = _rwkv_step((tr_s[0, j], tr_s[1, jb]), (rae_s[0, j], rae_s[1, jb]),
                                     (add_s[0, j], add_s[1, jb]), h)
            y_s[pl.ds(base_f + j * c, c), :] += yf
            y_s[pl.ds(base_b + jb * c, c), :] += yb
            yield
        return h

    def finish(_):
        inv_n = 1.0 / B_HEAD
        for t0 in range(0, seq, rb):
            rows = pl.ds(t0, rb)
            y = y_s[rows, :]
            mu = _split_dot(y, head, 0, 2) * inv_n
            yc = y - mu
            var = _split_dot(yc * yc, head, 0, 2) * inv_n
            yn = yc * lax.rsqrt(var + B_LNX_EPS) * lng_ref[...] + lnb_ref[...]
            g = g_ref[rows, :]
            o_ref[rows, :] = ((yn + bv_s[rows, :]) * (g * _sigmoid(g))).astype(o_ref.dtype)

    return n_it, jnp.zeros((LANES, 2 * LANES), F32), body, finish


def _rwkv_operands(proj, mu, w0, a0, wup, aup, k_k, k_a, r_k, lnx_g, lnx_b, seq, bw):
    nb = bw // LANES
    c0 = 4 * nb
    consts = _rwkv_consts(B_CHUNK, B_CUM)
    col = lambda off: pl.BlockSpec((seq, LANES), lambda b, g: (b, off + g))
    colf = lambda off: pl.BlockSpec((seq, LANES), lambda b, g: (b, off))
    par2 = lambda off: pl.BlockSpec((2, LANES), lambda b, g: (0, off + g))
    par2f = lambda off: pl.BlockSpec((2, LANES), lambda b, g: (0, off))
    par1 = pl.BlockSpec((1, LANES), lambda b, g: (0, g))
    up = pl.BlockSpec((2 * B_LORA, LANES), lambda b, g: (0, g))

    def fixed(a):
        return pl.BlockSpec(a.shape, lambda b, g: (0,) * a.ndim)

    row = lambda a: a.reshape(1, bw)
    in_specs = [col(c0), col(c0 + nb), col(c0 + 2 * nb), colf(c0 + 3 * nb), colf(c0 + 3 * nb + 1), col(c0 + 3 * nb + 2),
                par2(0), par2(nb), par2(2 * nb), par2f(3 * nb), par2f(3 * nb + 1),
                par2(0), par2(0), up, up, par1, par1, par1, par1, par1] + [fixed(a) for a in consts]
    operands = (proj, proj, proj, proj, proj, proj, mu, mu, mu, mu, mu, w0, a0,
                wup.reshape(2 * B_LORA, bw), aup.reshape(2 * B_LORA, bw),
                row(k_k), row(k_a), row(r_k), row(lnx_g), row(lnx_b), *consts)
    scratch = [pltpu.VMEM((seq, LANES), F32), pltpu.VMEM((seq, LANES), F32), pltpu.VMEM((seq, LANES), F32),
               pltpu.VMEM((2, seq, LANES), F32), pltpu.VMEM((2, seq, LANES), F32),
               pltpu.VMEM((2, seq, LANES), F32), pltpu.VMEM((seq, LANES), F32),
               pltpu.VMEM((seq, LANES), F32),
               pltpu.VMEM((2, B_GROUP, 2 * B_CHUNK, LANES), BF16),
               pltpu.VMEM((2, B_GROUP, B_CHUNK + LANES, LANES), BF16),
               pltpu.VMEM((2, B_GROUP, 2 * B_CHUNK + 2 * LANES, LANES), F32)]
    assert len(operands) == N_RWKV_IN and len(scratch) == N_RWKV_SCRATCH
    return in_specs, operands, scratch


def _recurrent_body(*refs):
    rwkv_in = refs[:N_RWKV_IN]
    hgrn_in = refs[N_RWKV_IN:N_RWKV_IN + N_HGRN_IN]
    rwkv_out, hgrn_out = refs[N_RWKV_IN + N_HGRN_IN:N_RWKV_IN + N_HGRN_IN + 2]
    scratch = refs[N_RWKV_IN + N_HGRN_IN + 2:]
    trips_b, carry_b, body_b, finish_b = _rwkv_parts(*rwkv_in, rwkv_out, *scratch[:N_RWKV_SCRATCH])
    trips_c, carry_c, body_c, finish_c = _hgrn_parts(*hgrn_in, hgrn_out, *scratch[N_RWKV_SCRATCH:])
    assert trips_b == trips_c

    def body(i, carry):
        return tuple(_interleave(body_b(i, carry[0]), body_c(i, carry[1])))

    carry_b, carry_c = lax.fori_loop(0, trips_b, body, (carry_b, carry_c))
    finish_b(carry_b)
    finish_c(carry_c)


def _recurrent(proj, rwkv_params, hgrn_params, bsz, seq, bw):
    t = proj.shape[0]
    nb = bw // LANES
    specs_b, ops_b, scratch_b = _rwkv_operands(proj, *rwkv_params, seq, bw)
    specs_c, ops_c, scratch_c = _hgrn_operands(proj, *hgrn_params, seq, bw)
    out_spec = pl.BlockSpec((seq, LANES), lambda b, g: (b, g))
    out_shape = jax.ShapeDtypeStruct((t, bw), BF16)
    return pl.pallas_call(
        _recurrent_body,
        grid=(bsz, nb),
        in_specs=specs_b + specs_c,
        out_specs=[out_spec, out_spec],
        out_shape=[out_shape, out_shape],
        scratch_shapes=scratch_b + scratch_c,
        compiler_params=_cparams(("parallel", "parallel")),
        name="rwkv7_hgrn2",
    )(*ops_b, *ops_c)


def kernel(x, rel_bias, pre_norm_g, post_norm_g, w_in, w_out, lambda_q1, lambda_k1, lambda_q2, lambda_k2, subln_g, rwkv_shift_mu, rwkv_w0, rwkv_w_up, rwkv_a0, rwkv_a_up, rwkv_k_k, rwkv_k_a, rwkv_r_k, rwkv_lnx_g, rwkv_lnx_b, hgrn_lb_logits, hgrn_norm_g):
    bsz, seq, d = x.shape
    depth = w_in.shape[0]
    bw = d // 2
    assert bw % (2 * LANES) == 0 and seq % (ATT_SUB * ATT_BLOCK) == 0 and seq % (C_GROUP * C_CHUNK) == 0
    assert seq % (B_GROUP * B_CHUNK) == 0
    assert w_in.shape[2] == 13 * bw + 4 * B_LORA
    x2 = x.reshape(bsz * seq, d)
    lb = jax.nn.softmax(hgrn_lb_logits.astype(F32), axis=1)
    lb = jnp.cumsum(lb, axis=1) - lb[:, :1]
    band = _bias_band(rel_bias, ATT_BLOCK)
    w_in_b, w_out_b = w_in.astype(BF16), w_out.astype(BF16)
    for l in range(depth):
        proj = _proj(x2, pre_norm_g[l], w_in_b, l)
        lam_p = jnp.stack([lambda_q1[l], lambda_k1[l], lambda_q2[l], lambda_k2[l]]).astype(F32)
        ya = _attn(proj, lam_p, band, subln_g[l], bsz, seq, bw, l)
        yb, yc = _recurrent(proj, (rwkv_shift_mu[l], rwkv_w0[l], rwkv_a0[l], rwkv_w_up[l], rwkv_a_up[l], rwkv_k_k[l],
                                   rwkv_k_a[l], rwkv_r_k[l], rwkv_lnx_g[l], rwkv_lnx_b[l]),
                            (lb[:, l], hgrn_norm_g[l]), bsz, seq, bw)
        x2 = _out(ya, yb, yc, w_out_b, l, post_norm_g[l], x2)
    return x2.reshape(bsz, seq, d)
```

```python
import functools
import math

import numpy as np
import jax
import jax.numpy as jnp
from jax import lax
from jax.experimental import pallas as pl
from jax.experimental.pallas import tpu as pltpu

F32 = jnp.float32
BF16 = jnp.bfloat16

LANES = 128
BF16_ROWS = 16
VMEM_LIMIT = 52 * 1024 * 1024

LOG2E = math.log2(math.e)

NORM_EPS = 1e-6
A_QK = 64
A_SUBLN_EPS = 1e-5
NUM_BUCKETS = 32
MAX_DISTANCE = 128
ATT_BLOCK = 256
ATT_SUB = 8
B_HEAD = 64
B_LORA = 64
B_LNX_EPS = 64e-5
B_CHUNK = 64
B_GROUP = 8
B_CUM = 4
C_CHUNK = 128
C_GROUP = 4
LB_FLOOR = 1e-30
ROW_BLOCK = 256

_NT = (((1,), (1,)), ((), ()))
_TN = (((0,), (0,)), ((), ()))


def _dot(a, b):
    return jnp.dot(a, b, preferred_element_type=F32)


def _dot_nt(a, b):
    return lax.dot_general(a, b, _NT, preferred_element_type=F32)


def _split_dot(a, b, split, passes):
    rem = a if split == 0 else b
    acc = None
    for _ in range(passes):
        piece = rem.astype(BF16)
        d = _dot(piece, b) if split == 0 else _dot(a, piece)
        acc = d if acc is None else acc + d
        rem = rem - piece.astype(F32)
    return acc


def _sigmoid(x):
    return 1.0 / (1.0 + jnp.exp(-x))


def _lockstep(*gens):
    results = [None] * len(gens)
    live = list(range(len(gens)))
    while live:
        for k in list(live):
            try:
                next(gens[k])
            except StopIteration as stop:
                results[k] = stop.value
                live.remove(k)
        yield
    return results


def _interleave(*gens):
    stepper = _lockstep(*gens)
    while True:
        try:
            next(stepper)
        except StopIteration as stop:
            return stop.value


def _aligned(x, m):
    return x if isinstance(x, int) else pl.multiple_of(x, m)


def _cparams(sem):
    return pltpu.CompilerParams(dimension_semantics=sem, vmem_limit_bytes=VMEM_LIMIT)


def _proj_body(x_ref, g_ref, w_ref, o_ref, u_s):
    @pl.when(pl.program_id(1) == 0)
    def _():
        xf = x_ref[...]
        ms = jnp.mean(xf * xf, axis=-1, keepdims=True)
        u_s[...] = (xf * lax.rsqrt(ms + NORM_EPS) * g_ref[...]).astype(BF16)

    o_ref[...] = _dot(u_s[...], w_ref[...])


def _proj(x2, g, w, layer, tm=1024, tn=1024):
    t, d = x2.shape
    p = w.shape[2]
    return pl.pallas_call(
        _proj_body,
        grid=(t // tm, pl.cdiv(p, tn)),
        in_specs=[pl.BlockSpec((tm, d), lambda i, j: (i, 0)),
                  pl.BlockSpec((1, d), lambda i, j: (0, 0)),
                  pl.BlockSpec((None, d, tn), lambda i, j: (layer, 0, j))],
        out_specs=pl.BlockSpec((tm, tn), lambda i, j: (i, j)),
        out_shape=jax.ShapeDtypeStruct((t, p), F32),
        scratch_shapes=[pltpu.VMEM((tm, d), BF16)],
        compiler_params=_cparams(("parallel", "arbitrary")),
        name="proj_in",
    )(x2, g.reshape(1, d), w)


def _out_body(ya_ref, yb_ref, yc_ref, w_ref, g_ref, x_ref, o_ref):
    bw = ya_ref.shape[1]
    m = (_dot(ya_ref[...], w_ref[0:bw, :]) + _dot(yb_ref[...], w_ref[bw:2 * bw, :])
         + _dot(yc_ref[...], w_ref[2 * bw:3 * bw, :]))
    ms = jnp.mean(m * m, axis=-1, keepdims=True)
    o_ref[...] = x_ref[...] + m * lax.rsqrt(ms + NORM_EPS) * g_ref[...]


def _out(ya, yb, yc, w, layer, g, x2, tm=256):
    t, d = x2.shape
    bw = ya.shape[1]
    row = lambda i: (i, 0)
    fixed = lambda i: (0, 0)
    return pl.pallas_call(
        _out_body,
        grid=(t // tm,),
        in_specs=[pl.BlockSpec((tm, bw), row), pl.BlockSpec((tm, bw), row), pl.BlockSpec((tm, bw), row),
                  pl.BlockSpec((None, 3 * bw, d), lambda i: (layer, 0, 0)), pl.BlockSpec((1, d), fixed),
                  pl.BlockSpec((tm, d), row)],
        out_specs=pl.BlockSpec((tm, d), row),
        out_shape=jax.ShapeDtypeStruct((t, d), F32),
        compiler_params=_cparams(("parallel",)),
        name="proj_out",
    )(ya, yb, yc, w, g.reshape(1, d), x2)


def _t5_bucket(rel):
    half = NUM_BUCKETS // 2
    max_exact = half // 2
    n = jnp.abs(rel)
    nf = jnp.maximum(n, max_exact).astype(F32)
    large = max_exact + (jnp.log(nf / max_exact) / math.log(MAX_DISTANCE / max_exact)
                         * (half - max_exact)).astype(jnp.int32)
    large = jnp.minimum(large, half - 1)
    return jnp.where(rel > 0, half, 0) + jnp.where(n < max_exact, n, large)


def _bias_band(rel_bias, blk):
    c = blk - jnp.arange(2 * blk, dtype=jnp.int32)
    far = jnp.full((2 * blk,), 2 * blk, jnp.int32)
    rel = jnp.stack([c, c + blk, c - blk, -far, far, far, far, far])
    return jnp.transpose(rel_bias.astype(F32)[_t5_bucket(rel)], (2, 0, 1))


def _attn_body(lam_ref, q_ref, k_ref, v_ref, g_ref, band_ref, sg_ref, o_ref, tile_s, kb_s, vt_s, s_s, *,
               lambda_init):
    b = pl.program_id(1)
    qi = pl.program_id(2)
    tq = tile_s.shape[1]
    nsub = q_ref.shape[0] // tq
    nk = k_ref.shape[0] // tq

    @pl.when((b == 0) & (qi == 0))
    def _():
        vec = band_ref[0] * LOG2E
        for d in range(3):
            w = jnp.broadcast_to(vec[d:d + 1, :], (tq, 2 * tq))
            tile_s[d] = pltpu.roll(w, tq, 1, stride=1, stride_axis=0)[:, :tq]
        for d in range(3, 5):
            tile_s[d] = jnp.broadcast_to(vec[d:d + 1, :tq], (tq, tq))

    @pl.when(qi == 0)
    def _():
        row = lax.broadcasted_iota(jnp.int32, (BF16_ROWS, k_ref.shape[0]), 0)
        vt_s[LANES:LANES + BF16_ROWS, :] = jnp.where(row == 0, 1.0, 0.0).astype(BF16)
        for kj in range(nk):
            rows = pl.ds(kj * tq, tq)
            kb_s[rows, :] = k_ref[rows, :].astype(BF16)
            vt_s[0:LANES, kj * tq:(kj + 1) * tq] = v_ref[rows, :].T.astype(BF16)

    lp = lam_ref[...]
    lam = (jnp.exp(jnp.sum(lp[0:1] * lp[1:2], axis=-1, keepdims=True))
           - jnp.exp(jnp.sum(lp[2:3] * lp[3:4], axis=-1, keepdims=True)) + lambda_init)

    lane = lax.broadcasted_iota(jnp.int32, (tq, LANES), 1)

    def pass1(sub):
        q = q_ref[pl.ds(sub * tq, tq), :] * (A_QK ** -0.5 * LOG2E)
        q2 = jnp.concatenate([jnp.where(lane < A_QK, q, 0.0), jnp.where(lane >= A_QK, q, 0.0)], axis=0).astype(BF16)
        mx = None
        for kj in range(nk):
            d = kj - (qi * nsub + sub)
            bias = tile_s[jnp.where(d == 0, 0, jnp.where(d == 1, 1, jnp.where(d == -1, 2, jnp.where(d > 0, 4, 3))))]
            s = _dot_nt(kb_s[pl.ds(kj * tq, tq), :], q2) + jnp.concatenate([bias, bias], axis=1)
            s_s[sub % 2, kj] = s
            blk_max = jnp.max(s.reshape(tq // 8, 8, 2 * tq), axis=0)
            mx = blk_max if mx is None else jnp.maximum(mx, blk_max)
            yield
        return jnp.max(mx, axis=0, keepdims=True)

    def pass2(sub, m):
        acc = None
        for kj in range(nk):
            p = jnp.exp2(s_s[sub % 2, kj] - m).astype(BF16)
            pv = _dot(vt_s[:, kj * tq:(kj + 1) * tq], p)
            acc = pv if acc is None else acc + pv
            yield
        out = acc[0:LANES] / acc[LANES:LANES + 1]
        att = (out[:, :tq] - lam * out[:, tq:]).T
        ms = jnp.mean(att * att, axis=-1, keepdims=True)
        y = att * lax.rsqrt(ms + A_SUBLN_EPS) * sg_ref[...] * (1.0 - lambda_init)
        g = g_ref[pl.ds(sub * tq, tq), :]
        o_ref[pl.ds(sub * tq, tq), :] = (y * (g * _sigmoid(g))).astype(o_ref.dtype)

    m = _interleave(pass1(0))[0]
    for sub in range(nsub):
        if sub + 1 < nsub:
            _, m = _interleave(pass2(sub, m), pass1(sub + 1))
        else:
            _interleave(pass2(sub, m))


def _attn(proj, lam_p, band, subln_g, bsz, seq, bw, layer_idx):
    t = proj.shape[0]
    nb = bw // LANES
    heads = bw // (2 * A_QK)
    tq = ATT_BLOCK
    tstep = ATT_SUB * tq
    nq = seq // tstep
    lambda_init = 0.8 - 0.6 * math.exp(-0.3 * layer_idx)
    return pl.pallas_call(
        functools.partial(_attn_body, lambda_init=lambda_init),
        grid=(heads, bsz, nq),
        in_specs=[pl.BlockSpec((4, A_QK), lambda h, b, qi: (0, 0)),
                  pl.BlockSpec((tstep, LANES), lambda h, b, qi: (b * nq + qi, h)),
                  pl.BlockSpec((seq, LANES), lambda h, b, qi: (b, nb + h)),
                  pl.BlockSpec((seq, LANES), lambda h, b, qi: (b, 2 * nb + h)),
                  pl.BlockSpec((tstep, LANES), lambda h, b, qi: (b * nq + qi, 3 * nb + h)),
                  pl.BlockSpec((1, 8, 2 * tq), lambda h, b, qi: (h, 0, 0)),
                  pl.BlockSpec((1, LANES), lambda h, b, qi: (0, 0))],
        out_specs=pl.BlockSpec((tstep, LANES), lambda h, b, qi: (b * nq + qi, h)),
        out_shape=jax.ShapeDtypeStruct((t, bw), BF16),
        scratch_shapes=[pltpu.VMEM((5, tq, tq), F32), pltpu.VMEM((seq, LANES), BF16),
                        pltpu.VMEM((LANES + BF16_ROWS, seq), BF16),
                        pltpu.VMEM((2, seq // tq, tq, 2 * tq), F32)],
        compiler_params=_cparams(("arbitrary", "arbitrary", "arbitrary")),
        name="diff_attn",
    )(lam_p, proj, proj, proj, proj, band, subln_g.reshape(1, LANES))


def _hgrn_consts(c):
    i = np.arange(c)
    lo = (i[:, None] >= i[None, :]).astype(np.float32)
    levels = int(math.log2(c))
    pmask = np.zeros((2, levels + 1, c, c), np.float32)
    pmask[:, 0] = np.eye(c)
    for l in range(levels):
        m = 1 << l
        same = (i[:, None] // (2 * m)) == (i[None, :] // (2 * m))
        second = (i % (2 * m)) >= m
        pmask[0, l + 1] = same & second[:, None] & ~second[None, :]
        pmask[1, l + 1] = same & ~second[:, None] & second[None, :]
    return (jnp.asarray(lo, BF16), jnp.asarray(lo.T, BF16), jnp.asarray(pmask))


def _boundary(cum, m, rev):
    c = cum.shape[0]
    if 2 * m >= 8:
        n = c // (2 * m)
        x = cum.reshape(n, 2 * m, LANES)
        r = m if rev else m - 1
        return jnp.broadcast_to(x[:, r:r + 1, :], x.shape).reshape(c, LANES)
    x = cum.reshape(c // 8, 8, LANES)
    sub = lax.broadcasted_iota(jnp.int32, x.shape, 1)
    out = None
    for node in range(8 // (2 * m)):
        r = node * 2 * m + (m if rev else m - 1)
        bc = jnp.broadcast_to(x[:, r:r + 1, :], x.shape)
        out = bc if out is None else jnp.where(sub >= node * 2 * m, bc, out)
    return out.reshape(c, LANES)


def _hgrn_prep(z, q, v, lbp, tri, pmask_ref, rev):
    lb_floor, one_m_lb, lb_pos = lbp
    c = z.shape[0]
    d = 1 if rev else 0
    e = jnp.exp(-jnp.abs(z))
    r = 1.0 / (1.0 + e)
    pos = z >= 0.0
    sig = jnp.where(pos, r, e * r)
    log_sig = jnp.minimum(z, 0.0) - jnp.log(1.0 + e)
    lf = jnp.where(lb_pos, jnp.log(lb_floor + one_m_lb * sig), log_sig)
    k = one_m_lb * jnp.where(pos, e * r, r)
    cum = _split_dot(tri, lf, 1, 3)
    tot = cum[0:1] if rev else cum[c - 1:c]
    vb = v.astype(BF16)
    scores = _dot_nt(q.astype(BF16), k.astype(BF16)) * pmask_ref[d, 0]
    for l in range(int(math.log2(c))):
        e = jnp.exp(-jnp.abs(cum - _boundary(cum, 1 << l, rev)))
        scores = scores + _dot_nt((q * e).astype(BF16), (k * e).astype(BF16)) * pmask_ref[d, l + 1]
        if l == 3:
            yield
    o = _dot(scores.astype(BF16), vb)
    qd = (q * jnp.exp(cum)).astype(BF16)
    kv_t = _dot(v.T.astype(BF16), (k * jnp.exp(tot - cum)).astype(BF16))
    return o, qd, kv_t, jnp.exp(tot)


N_HGRN_IN = 10


def _hgrn_parts(q_ref, v_ref, zf_ref, zb_ref, g_ref, lb_ref, ng_ref, lo_ref, up_ref, pmask_ref,
                o_ref, acc_s):
    seq = q_ref.shape[0]
    c = lo_ref.shape[0]
    n = seq // c
    group = C_GROUP
    lbs = lb_ref[...]

    def lb_params(d):
        lb = lbs[d:d + 1]
        return jnp.maximum(lb, LB_FLOOR), 1.0 - lb, lb > 0.0

    lbp = (lb_params(0), lb_params(1))
    tri = (lo_ref[...], up_ref[...])
    z_ref = (zf_ref, zb_ref)
    acc_s[...] = jnp.zeros_like(acc_s)

    def body(i, carry):
        st = list(carry)
        base = (_aligned(i * (group * c), group * c), _aligned((n // group - 1 - i) * (group * c), group * c))
        prep = []
        for d in range(2):
            for j in range(group):
                rows = pl.ds(base[d] + j * c, c)
                prep.append((yield from _hgrn_prep(z_ref[d][rows, :], q_ref[rows, :], v_ref[rows, :], lbp[d], tri[d],
                                                   pmask_ref, d == 1)))
                yield
        for step in range(group):
            for d in range(2):
                j = step if d == 0 else group - 1 - step
                o, qd, kv_t, dec = prep[d * group + j]
                rows = pl.ds(base[d] + j * c, c)
                acc_s[rows, :] += o + _dot_nt(qd, st[d].astype(BF16))
                st[d] = st[d] * dec + kv_t
            yield
        return tuple(st)

    def finish(_):
        for r0 in range(0, seq, ROW_BLOCK):
            rows = pl.ds(r0, ROW_BLOCK)
            o = acc_s[rows, :]
            ms = jnp.mean(o * o, axis=-1, keepdims=True)
            g = g_ref[rows, :]
            o_ref[rows, :] = (o * lax.rsqrt(ms + NORM_EPS) * ng_ref[...] * (g * _sigmoid(g))).astype(o_ref.dtype)

    zero = jnp.zeros((LANES, LANES), F32)
    return n // group, (zero, zero), body, finish


def _hgrn_operands(proj, lb_l, norm_g, seq, bw):
    nb = bw // LANES
    c0 = 8 * nb + 2
    consts = _hgrn_consts(C_CHUNK)
    col = lambda off: pl.BlockSpec((seq, LANES), lambda b, h: (b, off + h))
    fixed2 = lambda shape: pl.BlockSpec(shape, lambda b, h: (0, 0))
    in_specs = [col(c0), col(c0 + nb), col(c0 + 2 * nb), col(c0 + 3 * nb), col(c0 + 4 * nb),
                pl.BlockSpec((2, LANES), lambda b, h: (0, h)), fixed2((1, LANES)),
                fixed2(consts[0].shape), fixed2(consts[1].shape),
                pl.BlockSpec(consts[2].shape, lambda b, h: (0, 0, 0, 0))]
    operands = (proj, proj, proj, proj, proj, lb_l, norm_g.reshape(1, LANES), *consts)
    assert len(operands) == N_HGRN_IN
    return in_specs, operands, [pltpu.VMEM((seq, LANES), F32)]


def _rwkv_consts(c, group):
    i = np.arange(c)
    j = np.arange(group * c)
    lo = ((j[:, None] >= j[None, :]) & (j[:, None] // c == j[None, :] // c)).astype(np.float32)
    tmask = np.stack([i[:, None] > i[None, :], i[:, None] >= i[None, :],
                      i[:, None] < i[None, :], i[:, None] <= i[None, :]]).astype(np.float32)
    levels = int(math.log2(c))
    lvl = np.zeros((levels, c, c), np.float32)
    for l in range(levels):
        m = 1 << l
        lvl[l] = ((i[:, None] // (2 * m)) == (i[None, :] // (2 * m))) & ((i[:, None] // m) != (i[None, :] // m))
    head = (np.arange(LANES)[:, None] // B_HEAD) == (np.arange(LANES)[None, :] // B_HEAD)
    tile4 = lambda a: np.concatenate([a] * 4, axis=-1)
    quarter = np.stack([np.broadcast_to((np.arange(4 * c) // c) == q, (c, 4 * c)) for q in range(4)])
    return (jnp.asarray(lo, BF16), jnp.asarray(lo.T, BF16), jnp.asarray(tile4(tmask)), jnp.asarray(tile4(lvl)),
            jnp.asarray(head.astype(np.float32), BF16), jnp.asarray(tile4(np.eye(c, dtype=np.float32))),
            jnp.asarray(quarter.astype(np.float32), BF16))


def _shifted(ref, mu, t0, rows, seq):
    x = ref[pl.ds(t0, rows), :]
    row = lax.broadcasted_iota(jnp.int32, x.shape, 0)
    if t0 == 0:
        prev = jnp.where(row == 0, 0.0, pltpu.roll(x, 1, 0))
    else:
        prev = ref[pl.ds(t0 - 1, rows), :]
    if t0 + rows == seq:
        nxt = jnp.where(row == rows - 1, 0.0, pltpu.roll(x, rows - 1, 0))
    else:
        nxt = ref[pl.ds(t0 + 1, rows), :]
    return x + mu[0:1] * (prev - x) + mu[1:2] * (nxt - x)


def _stack_heads(x, lane_lo):
    return jnp.concatenate([jnp.where(lane_lo, x, 0.0), jnp.where(lane_lo, 0.0, x)], axis=0)


def _rwkv_prep(r, lw, kk, kka, v, km, tri, tmask_ref, lvl_ref, eye_ref, quarter_ref, hmask, rev, group):
    c = r.shape[0] // group
    slab = tri.shape[0]
    lane_lo = lax.broadcasted_iota(jnp.int32, (c, LANES), 1) < B_HEAD
    row_lo = lax.broadcasted_iota(jnp.int32, (LANES, c), 0) < B_HEAD
    cw_all = jnp.concatenate([_split_dot(tri, lw[i:i + slab], 1, 3) for i in range(0, group * c, slab)], axis=0)
    strict = tmask_ref[2 if rev else 0]
    incl = tmask_ref[3 if rev else 1]
    eye = eye_ref[...]
    rows = [slice(i * c, (i + 1) * c) for i in range(group)]
    cw = [cw_all[s] for s in rows]
    tot = [x[0:1] if rev else x[c - 1:c] for x in cw]
    e_neg = [jnp.exp(-x) for x in cw]
    e_tot = [jnp.exp(t - x) for t, x in zip(tot, cw)]
    heads = lambda x: _stack_heads(x, lane_lo)
    bd = [kk[s] * jnp.exp(x - lw[s]) for s, x in zip(rows, cw)]
    ad = [-(kka[s] * e) for s, e in zip(rows, e_neg)]
    kd = [km[s] * e for s, e in zip(rows, e_neg)]
    rd = [r[s] * jnp.exp(x) for s, x in zip(rows, cw)]
    ae_t = [(-(kka[s] * e)).T for s, e in zip(rows, e_tot)]
    ke_t = [(km[s] * e).T.astype(BF16) for s, e in zip(rows, e_tot)]
    v2 = [heads(v[s]).astype(BF16) for s in rows]
    dcol = [jnp.broadcast_to(jnp.exp(t), (LANES, LANES)).T for t in tot]
    kv = [_dot(k, v[s].astype(BF16)) * hmask for k, s in zip(ke_t, rows)]
    yield
    pairs = [(2 * i, 2 * i + 1) for i in range(group // 2)]
    side = lambda xs: [jnp.concatenate([xs[a], xs[b]], axis=1) for a, b in pairs]
    lhs = side([jnp.concatenate([b, q], axis=0).astype(BF16) for b, q in zip(bd, rd)])
    rhs = [jnp.concatenate([heads(ad[ch]), heads(kd[ch])], axis=0).astype(BF16) for ch in range(group)]
    gram = [_dot_nt(x, _block_diag(rhs[a], rhs[b])) for x, (a, b) in zip(lhs, pairs)]
    pick = lambda g, i, j: jnp.concatenate([g[i * c:(i + 1) * c, j * LANES:(j + 1) * LANES],
                                            g[i * c:(i + 1) * c, (2 + j) * LANES:(3 + j) * LANES]], axis=1)
    ba = [pick(g, 0, 0) * strict for g in gram]
    bk = [pick(g, 0, 1) * strict for g in gram]
    ra = [pick(g, 1, 0) * incl for g in gram]
    rk = [pick(g, 1, 1) * incl for g in gram]
    yield
    spread = lambda x: jnp.concatenate([x * quarter_ref[q] for q in range(4)], axis=0)
    d = [eye + x * lvl_ref[0] for x in ba]
    for l in range(1, int(math.log2(c))):
        lv = lvl_ref[l]
        db = [x.astype(BF16) for x in d]
        dm = [_dot(x, spread((a * lv).astype(BF16))).astype(BF16) for x, a in zip(db, ba)]
        d = [x + _dot(y, spread(z)) for x, y, z in zip(d, dm, db)]
        yield
    xv = [_dot(jnp.concatenate([x, y], axis=0).astype(BF16), _block_diag(v2[a], v2[b]))
          for x, y, (a, b) in zip(bk, rk, pairs)]
    yield
    zrows = lambda y, ch: heads2_wide(jnp.concatenate([y, bd[ch]], axis=1)).astype(BF16)
    lane_lo2 = jnp.concatenate([lane_lo, lane_lo], axis=1)
    heads2_wide = lambda z: jnp.concatenate([jnp.where(lane_lo2, z, 0.0), jnp.where(lane_lo2, 0.0, z)], axis=0)
    tz = [_dot(x.astype(BF16), _block_diag(zrows(y[:c, :LANES], a), zrows(y[:c, LANES:], b)))
          for x, y, (a, b) in zip(d, xv, pairs)]
    tr, rae, add = [], [], []
    for i, (a, b) in enumerate(pairs):
        for h, ch in enumerate((a, b)):
            t = tz[i][:, 2 * h * LANES:(2 * h + 2) * LANES]
            y = xv[i][:, h * LANES:(h + 1) * LANES]
            aet = jnp.concatenate([jnp.where(row_lo, ae_t[ch], 0.0), jnp.where(row_lo, 0.0, ae_t[ch])], axis=1)
            tr.append(jnp.concatenate([t[:, LANES:], rd[ch]], axis=0).astype(BF16))
            rae.append(jnp.concatenate([ra[i][:, h * LANES:(h + 1) * LANES], aet], axis=0).astype(BF16))
            add.append(jnp.concatenate([t[:, :LANES], y[c:], kv[ch], dcol[ch]], axis=0))
    return tr, rae, add


def _block_diag(x, y):
    z = jnp.zeros_like(x)
    return jnp.concatenate([jnp.concatenate([x, z], axis=1), jnp.concatenate([z, y], axis=1)], axis=0)


def _rwkv_step(tr, rae, add, h):
    c = tr[0].shape[0] // 2
    lane_lo = lax.broadcasted_iota(jnp.int32, (c, LANES), 1) < B_HEAD
    side = lambda f, b: jnp.concatenate([f, b], axis=1)
    lanes = lambda x, d: x[:, d * LANES:(d + 1) * LANES]
    hb = h.astype(BF16)
    p = _dot(side(*tr), _block_diag(hb[:, :LANES], hb[:, LANES:]))
    u = [_stack_heads(add[d][0:c] + lanes(p[:c], d), lane_lo).astype(BF16) for d in range(2)]
    qq = _dot(side(*rae), _block_diag(*u))
    ys = [lanes(p[c:], d) + lanes(qq[:c], d) + add[d][c:2 * c] for d in range(2)]
    kv, dec = slice(2 * c, 2 * c + LANES), slice(2 * c + LANES, 2 * c + 2 * LANES)
    h = side(add[0][dec], add[1][dec]) * h + qq[c:] + side(add[0][kv], add[1][kv])
    return ys, h


N_RWKV_IN = 27
N_RWKV_SCRATCH = 11


def _rwkv_parts(r_ref, k_ref, v_ref, wd_ref, ad_ref, g_ref, mur_ref, muk_ref, muv_ref, muw_ref, mua_ref,
                w0_ref, a0_ref, wup_ref, aup_ref, kk_ref, ka_ref, rk_ref, lng_ref, lnb_ref,
                lo_ref, up_ref, tmask_ref, lvl_ref, head_ref, eye_ref, quarter_ref,
                o_ref, r_s, v_s, kk_s, lw_s, kka_s, km_s, bv_s, y_s, tr_s, rae_s, add_s):
    seq = r_ref.shape[0]
    c = B_CHUNK
    n = seq // c
    rb = ROW_BLOCK
    group = B_GROUP
    gc = group * c
    n_it = n // group
    head = head_ref[...]
    hmask = head.astype(F32)
    lane_lo = lax.broadcasted_iota(jnp.int32, (rb, LANES), 1) < B_LORA

    def prologue(t0):
        rows = pl.ds(t0, rb)
        r = _shifted(r_ref, mur_ref[...], t0, rb, seq)
        k = _shifted(k_ref, muk_ref[...], t0, rb, seq)
        v = _shifted(v_ref, muv_ref[...], t0, rb, seq)
        yield
        twd = jnp.tanh(_shifted(wd_ref, muw_ref[...], t0, rb, seq))
        ad = _shifted(ad_ref, mua_ref[...], t0, rb, seq)
        kk = k * kk_ref[...]
        kk = kk / jnp.maximum(jnp.sqrt(_split_dot(kk * kk, head, 0, 2)), 1e-12)
        kmsum = None
        for d in range(2):
            yield
            sel = lane_lo if d == 0 else jnp.logical_not(lane_lo)
            wl = _dot(jnp.where(sel, twd, 0.0).astype(BF16), wup_ref[...].astype(BF16))
            al = _dot(jnp.where(sel, ad, 0.0).astype(BF16), aup_ref[...].astype(BF16))
            lw_s[d, rows, :] = -math.exp(-0.5) * _sigmoid(w0_ref[d:d + 1, :] + wl)
            a = _sigmoid(a0_ref[d:d + 1, :] + al)
            km = k * (1.0 + (a - 1.0) * ka_ref[...])
            kka_s[d, rows, :] = kk * a
            km_s[d, rows, :] = km
            kmsum = km if kmsum is None else kmsum + km
        bonus = _split_dot(r * kmsum * rk_ref[...], head, 0, 2)
        r_s[rows, :] = r
        v_s[rows, :] = v
        kk_s[rows, :] = kk
        bv_s[rows, :] = bonus * v
        yield

    first = [t0 for t0 in range(0, seq, rb) if t0 < gc or t0 >= seq - gc]
    later = [t0 for t0 in range(0, seq, rb) if t0 not in first]
    for t0 in first:
        _interleave(prologue(t0))

    def prologue_rest():
        for t0 in later:
            yield from prologue(t0)

    y_s[...] = jnp.zeros_like(y_s)
    tri_lo, tri_up = lo_ref[...], up_ref[...]

    def bases(i):
        return _aligned(i * gc, gc), _aligned((n_it - 1 - i) * gc, gc)

    def prep(d, base, tri):
        rows = pl.ds(base, gc)
        tr, rae, add = yield from _rwkv_prep(
            r_s[rows, :], lw_s[d, rows, :], kk_s[rows, :], kka_s[d, rows, :], v_s[rows, :], km_s[d, rows, :],
            tri, tmask_ref, lvl_ref, eye_ref, quarter_ref, hmask, d == 1, group)
        for j in range(group):
            tr_s[d, j] = tr[j]
            rae_s[d, j] = rae[j]
            add_s[d, j] = add[j]

    def body(i, h):
        base_f, base_b = bases(i)
        yield from _lockstep(prep(0, base_f, tri_lo), prep(1, base_b, tri_up))
        for j in range(group):
            jb = group - 1 - j
            (yf, yb), h = _rwkv_step((tr_s[0, j], tr_s[1, jb]), (rae_s[0, j], rae_s[1, jb]),
                                     (add_s[0, j], add_s[1, jb]), h)
            y_s[pl.ds(base_f + j * c, c), :] += yf
            y_s[pl.ds(base_b + jb * c, c), :] += yb
            yield
        return h

    def finish(_):
        inv_n = 1.0 / B_HEAD
        for t0 in range(0, seq, rb):
            rows = pl.ds(t0, rb)
            y = y_s[rows, :]
            mu = _split_dot(y, head, 0, 2) * inv_n
            yc = y - mu
            var = _split_dot(yc * yc, head, 0, 2) * inv_n
            yn = yc * lax.rsqrt(var + B_LNX_EPS) * lng_ref[...] + lnb_ref[...]
            g = g_ref[rows, :]
            o_ref[rows, :] = ((yn + bv_s[rows, :]) * (g * _sigmoid(g))).astype(o_ref.dtype)

    return n_it, jnp.zeros((LANES, 2 * LANES), F32), body, finish, prologue_rest()


def _rwkv_operands(proj, mu, w0, a0, wup, aup, k_k, k_a, r_k, lnx_g, lnx_b, seq, bw):
    nb = bw // LANES
    c0 = 4 * nb
    consts = _rwkv_consts(B_CHUNK, B_CUM)
    col = lambda off: pl.BlockSpec((seq, LANES), lambda b, g: (b, off + g))
    colf = lambda off: pl.BlockSpec((seq, LANES), lambda b, g: (b, off))
    par2 = lambda off: pl.BlockSpec((2, LANES), lambda b, g: (0, off + g))
    par2f = lambda off: pl.BlockSpec((2, LANES), lambda b, g: (0, off))
    par1 = pl.BlockSpec((1, LANES), lambda b, g: (0, g))
    up = pl.BlockSpec((2 * B_LORA, LANES), lambda b, g: (0, g))

    def fixed(a):
        return pl.BlockSpec(a.shape, lambda b, g: (0,) * a.ndim)

    row = lambda a: a.reshape(1, bw)
    in_specs = [col(c0), col(c0 + nb), col(c0 + 2 * nb), colf(c0 + 3 * nb), colf(c0 + 3 * nb + 1), col(c0 + 3 * nb + 2),
                par2(0), par2(nb), par2(2 * nb), par2f(3 * nb), par2f(3 * nb + 1),
                par2(0), par2(0), up, up, par1, par1, par1, par1, par1] + [fixed(a) for a in consts]
    operands = (proj, proj, proj, proj, proj, proj, mu, mu, mu, mu, mu, w0, a0,
                wup.reshape(2 * B_LORA, bw), aup.reshape(2 * B_LORA, bw),
                row(k_k), row(k_a), row(r_k), row(lnx_g), row(lnx_b), *consts)
    scratch = [pltpu.VMEM((seq, LANES), F32), pltpu.VMEM((seq, LANES), F32), pltpu.VMEM((seq, LANES), F32),
               pltpu.VMEM((2, seq, LANES), F32), pltpu.VMEM((2, seq, LANES), F32),
               pltpu.VMEM((2, seq, LANES), F32), pltpu.VMEM((seq, LANES), F32),
               pltpu.VMEM((seq, LANES), F32),
               pltpu.VMEM((2, B_GROUP, 2 * B_CHUNK, LANES), BF16),
               pltpu.VMEM((2, B_GROUP, B_CHUNK + LANES, LANES), BF16),
               pltpu.VMEM((2, B_GROUP, 2 * B_CHUNK + 2 * LANES, LANES), F32)]
    assert len(operands) == N_RWKV_IN and len(scratch) == N_RWKV_SCRATCH
    return in_specs, operands, scratch


def _recurrent_body(*refs):
    rwkv_in = refs[:N_RWKV_IN]
    hgrn_in = refs[N_RWKV_IN:N_RWKV_IN + N_HGRN_IN]
    rwkv_out, hgrn_out = refs[N_RWKV_IN + N_HGRN_IN:N_RWKV_IN + N_HGRN_IN + 2]
    scratch = refs[N_RWKV_IN + N_HGRN_IN + 2:]
    trips_b, carry_b, body_b, finish_b, prologue_rest = _rwkv_parts(*rwkv_in, rwkv_out, *scratch[:N_RWKV_SCRATCH])
    trips_c, carry_c, body_c, finish_c = _hgrn_parts(*hgrn_in, hgrn_out, *scratch[N_RWKV_SCRATCH:])
    assert trips_b == trips_c and trips_b >= 2

    def body(i, carry):
        return tuple(_interleave(body_b(i, carry[0]), body_c(i, carry[1])))

    carry_b, carry_c, _ = _interleave(body_b(0, carry_b), body_c(0, carry_c), prologue_rest)
    carry_b, carry_c = lax.fori_loop(1, trips_b, body, (carry_b, carry_c))
    finish_b(carry_b)
    finish_c(carry_c)


def _recurrent(proj, rwkv_params, hgrn_params, bsz, seq, bw):
    t = proj.shape[0]
    nb = bw // LANES
    specs_b, ops_b, scratch_b = _rwkv_operands(proj, *rwkv_params, seq, bw)
    specs_c, ops_c, scratch_c = _hgrn_operands(proj, *hgrn_params, seq, bw)
    out_spec = pl.BlockSpec((seq, LANES), lambda b, g: (b, g))
    out_shape = jax.ShapeDtypeStruct((t, bw), BF16)
    return pl.pallas_call(
        _recurrent_body,
        grid=(bsz, nb),
        in_specs=specs_b + specs_c,
        out_specs=[out_spec, out_spec],
        out_shape=[out_shape, out_shape],
        scratch_shapes=scratch_b + scratch_c,
        compiler_params=_cparams(("parallel", "parallel")),
        name="rwkv7_hgrn2",
    )(*ops_b, *ops_c)


def kernel(x, rel_bias, pre_norm_g, post_norm_g, w_in, w_out, lambda_q1, lambda_k1, lambda_q2, lambda_k2, subln_g, rwkv_shift_mu, rwkv_w0, rwkv_w_up, rwkv_a0, rwkv_a_up, rwkv_k_k, rwkv_k_a, rwkv_r_k, rwkv_lnx_g, rwkv_lnx_b, hgrn_lb_logits, hgrn_norm_g):
    bsz, seq, d = x.shape
    depth = w_in.shape[0]
    bw = d // 2
    assert bw % (2 * LANES) == 0 and seq % (ATT_SUB * ATT_BLOCK) == 0 and seq % (C_GROUP * C_CHUNK) == 0
    assert seq % (B_GROUP * B_CHUNK) == 0
    assert w_in.shape[2] == 13 * bw + 4 * B_LORA
    x2 = x.reshape(bsz * seq, d)
    lb = jax.nn.softmax(hgrn_lb_logits.astype(F32), axis=1)
    lb = jnp.cumsum(lb, axis=1) - lb[:, :1]
    band = _bias_band(rel_bias, ATT_BLOCK)
    w_in_b, w_out_b = w_in.astype(BF16), w_out.astype(BF16)
    for l in range(depth):
        proj = _proj(x2, pre_norm_g[l], w_in_b, l)
        lam_p = jnp.stack([lambda_q1[l], lambda_k1[l], lambda_q2[l], lambda_k2[l]]).astype(F32)
        ya = _attn(proj, lam_p, band, subln_g[l], bsz, seq, bw, l)
        yb, yc = _recurrent(proj, (rwkv_shift_mu[l], rwkv_w0[l], rwkv_a0[l], rwkv_w_up[l], rwkv_a_up[l], rwkv_k_k[l],
                                   rwkv_k_a[l], rwkv_r_k[l], rwkv_lnx_g[l], rwkv_lnx_b[l]),
                            (lb[:, l], hgrn_norm_g[l]), bsz, seq, bw)
        x2 = _out(ya, yb, yc, w_out_b, l, post_norm_g[l], x2)
    return x2.reshape(bsz, seq, d)
```

```python
import functools
import math

import numpy as np
import jax
import jax.numpy as jnp
from jax import lax
from jax.experimental import pallas as pl
from jax.experimental.pallas import tpu as pltpu

F32 = jnp.float32
BF16 = jnp.bfloat16

LANES = 128
SUBLANES = 8
BF16_ROWS = 16
VMEM_LIMIT = 52 * 1024 * 1024

LOG2E = math.log2(math.e)

NORM_EPS = 1e-6
A_QK = 64
A_SUBLN_EPS = 1e-5
NUM_BUCKETS = 32
MAX_DISTANCE = 128
ATT_BLOCK = 256
ATT_SUB = 8
B_HEAD = 64
B_LORA = 64
B_LNX_EPS = 64e-5
B_CHUNK = 64
B_GROUP = 8
B_CUM = 4
C_CHUNK = 128
C_GROUP = 4
LB_FLOOR = 1e-30
ROW_BLOCK = 256

_NT = (((1,), (1,)), ((), ()))


def _dot(a, b):
    return jnp.dot(a, b, preferred_element_type=F32)


def _dot_nt(a, b):
    return lax.dot_general(a, b, _NT, preferred_element_type=F32)


def _split_dot(a, b, split, passes):
    rem = a if split == 0 else b
    acc = None
    for _ in range(passes):
        piece = rem.astype(BF16)
        d = _dot(piece, b) if split == 0 else _dot(a, piece)
        acc = d if acc is None else acc + d
        rem = rem - piece.astype(F32)
    return acc


def _sigmoid(x):
    return 1.0 / (1.0 + jnp.exp(-x))


def _lockstep(*gens):
    results = [None] * len(gens)
    live = list(range(len(gens)))
    while live:
        for k in list(live):
            try:
                next(gens[k])
            except StopIteration as stop:
                results[k] = stop.value
                live.remove(k)
        yield
    return results


def _interleave(*gens):
    stepper = _lockstep(*gens)
    while True:
        try:
            next(stepper)
        except StopIteration as stop:
            return stop.value


def _aligned(x, m):
    return x if isinstance(x, int) else pl.multiple_of(x, m)


def _cparams(sem):
    return pltpu.CompilerParams(dimension_semantics=sem, vmem_limit_bytes=VMEM_LIMIT)


def _proj_body(x_ref, g_ref, w_ref, o_ref, u_s):
    @pl.when(pl.program_id(1) == 0)
    def _():
        xf = x_ref[...]
        ms = jnp.mean(xf * xf, axis=-1, keepdims=True)
        u_s[...] = (xf * lax.rsqrt(ms + NORM_EPS) * g_ref[...]).astype(BF16)

    o_ref[...] = _dot(u_s[...], w_ref[...])


def _proj(x2, g, w, layer, tm=1024, tn=1024):
    t, d = x2.shape
    p = w.shape[2]
    return pl.pallas_call(
        _proj_body,
        grid=(t // tm, pl.cdiv(p, tn)),
        in_specs=[pl.BlockSpec((tm, d), lambda i, j: (i, 0)),
                  pl.BlockSpec((1, d), lambda i, j: (0, 0)),
                  pl.BlockSpec((None, d, tn), lambda i, j: (layer, 0, j))],
        out_specs=pl.BlockSpec((tm, tn), lambda i, j: (i, j)),
        out_shape=jax.ShapeDtypeStruct((t, p), F32),
        scratch_shapes=[pltpu.VMEM((tm, d), BF16)],
        compiler_params=_cparams(("parallel", "arbitrary")),
        name="proj_in",
    )(x2, g.reshape(1, d), w)


def _out_body(ya_ref, yb_ref, yc_ref, w_ref, g_ref, x_ref, o_ref):
    bw = ya_ref.shape[1]
    m = (_dot(ya_ref[...], w_ref[0:bw, :]) + _dot(yb_ref[...], w_ref[bw:2 * bw, :])
         + _dot(yc_ref[...], w_ref[2 * bw:3 * bw, :]))
    ms = jnp.mean(m * m, axis=-1, keepdims=True)
    o_ref[...] = x_ref[...] + m * lax.rsqrt(ms + NORM_EPS) * g_ref[...]


def _out(ya, yb, yc, w, layer, g, x2, tm=256):
    t, d = x2.shape
    bw = ya.shape[1]
    row = lambda i: (i, 0)
    fixed = lambda i: (0, 0)
    return pl.pallas_call(
        _out_body,
        grid=(t // tm,),
        in_specs=[pl.BlockSpec((tm, bw), row), pl.BlockSpec((tm, bw), row), pl.BlockSpec((tm, bw), row),
                  pl.BlockSpec((None, 3 * bw, d), lambda i: (layer, 0, 0)), pl.BlockSpec((1, d), fixed),
                  pl.BlockSpec((tm, d), row)],
        out_specs=pl.BlockSpec((tm, d), row),
        out_shape=jax.ShapeDtypeStruct((t, d), F32),
        compiler_params=_cparams(("parallel",)),
        name="proj_out",
    )(ya, yb, yc, w, g.reshape(1, d), x2)


def _t5_bucket(rel):
    half = NUM_BUCKETS // 2
    max_exact = half // 2
    n = jnp.abs(rel)
    nf = jnp.maximum(n, max_exact).astype(F32)
    large = max_exact + (jnp.log(nf / max_exact) / math.log(MAX_DISTANCE / max_exact)
                         * (half - max_exact)).astype(jnp.int32)
    large = jnp.minimum(large, half - 1)
    return jnp.where(rel > 0, half, 0) + jnp.where(n < max_exact, n, large)


def _bias_band(rel_bias, blk):
    c = blk - jnp.arange(2 * blk, dtype=jnp.int32)
    far = jnp.full((2 * blk,), 2 * blk, jnp.int32)
    rel = jnp.stack([c, c + blk, c - blk, -far, far, far, far, far])
    return jnp.transpose(rel_bias.astype(F32)[_t5_bucket(rel)], (2, 0, 1))


def _attn_body(lam_ref, q_ref, k_ref, v_ref, g_ref, band_ref, sg_ref, o_ref, tile_s, kb_s, vt_s, s_s, *,
               lambda_init):
    b = pl.program_id(1)
    qi = pl.program_id(2)
    tq = tile_s.shape[1]
    nsub = q_ref.shape[0] // tq
    nk = k_ref.shape[0] // tq

    @pl.when((b == 0) & (qi == 0))
    def _():
        vec = band_ref[0] * LOG2E
        for d in range(3):
            w = jnp.broadcast_to(vec[d:d + 1, :], (tq, 2 * tq))
            tile_s[d] = pltpu.roll(w, tq, 1, stride=1, stride_axis=0)[:, :tq]
        for d in range(3, 5):
            tile_s[d] = jnp.broadcast_to(vec[d:d + 1, :tq], (tq, tq))

    @pl.when(qi == 0)
    def _():
        row = lax.broadcasted_iota(jnp.int32, (BF16_ROWS, k_ref.shape[0]), 0)
        vt_s[LANES:LANES + BF16_ROWS, :] = jnp.where(row == 0, 1.0, 0.0).astype(BF16)
        for kj in range(nk):
            rows = pl.ds(kj * tq, tq)
            kb_s[rows, :] = k_ref[rows, :].astype(BF16)
            vt_s[0:LANES, kj * tq:(kj + 1) * tq] = v_ref[rows, :].T.astype(BF16)

    lp = lam_ref[...]
    lam = (jnp.exp(jnp.sum(lp[0:1] * lp[1:2], axis=-1, keepdims=True))
           - jnp.exp(jnp.sum(lp[2:3] * lp[3:4], axis=-1, keepdims=True)) + lambda_init)

    lane = lax.broadcasted_iota(jnp.int32, (tq, LANES), 1)

    def pass1(sub):
        q = q_ref[pl.ds(sub * tq, tq), :] * (A_QK ** -0.5 * LOG2E)
        q2 = jnp.concatenate([jnp.where(lane < A_QK, q, 0.0), jnp.where(lane >= A_QK, q, 0.0)], axis=0).astype(BF16)
        mx = None
        for kj in range(nk):
            d = kj - (qi * nsub + sub)
            bias = tile_s[jnp.where(d == 0, 0, jnp.where(d == 1, 1, jnp.where(d == -1, 2, jnp.where(d > 0, 4, 3))))]
            s = _dot_nt(kb_s[pl.ds(kj * tq, tq), :], q2) + jnp.concatenate([bias, bias], axis=1)
            s_s[sub % 2, kj] = s
            blk_max = jnp.max(s.reshape(tq // SUBLANES, SUBLANES, 2 * tq), axis=0)
            mx = blk_max if mx is None else jnp.maximum(mx, blk_max)
            yield
        return jnp.max(mx, axis=0, keepdims=True)

    def pass2(sub, m):
        acc = None
        for kj in range(nk):
            p = jnp.exp2(s_s[sub % 2, kj] - m).astype(BF16)
            pv = _dot(vt_s[:, kj * tq:(kj + 1) * tq], p)
            acc = pv if acc is None else acc + pv
            yield
        out = acc[0:LANES] / acc[LANES:LANES + 1]
        att = (out[:, :tq] - lam * out[:, tq:]).T
        ms = jnp.mean(att * att, axis=-1, keepdims=True)
        y = att * lax.rsqrt(ms + A_SUBLN_EPS) * sg_ref[...] * (1.0 - lambda_init)
        g = g_ref[pl.ds(sub * tq, tq), :]
        o_ref[pl.ds(sub * tq, tq), :] = (y * (g * _sigmoid(g))).astype(o_ref.dtype)

    m = _interleave(pass1(0))[0]
    for sub in range(nsub):
        if sub + 1 < nsub:
            _, m = _interleave(pass2(sub, m), pass1(sub + 1))
        else:
            _interleave(pass2(sub, m))


def _attn(proj, lam_p, band, subln_g, bsz, seq, bw, layer_idx):
    t = proj.shape[0]
    nb = bw // LANES
    heads = bw // (2 * A_QK)
    tq = ATT_BLOCK
    tstep = ATT_SUB * tq
    nq = seq // tstep
    lambda_init = 0.8 - 0.6 * math.exp(-0.3 * layer_idx)
    return pl.pallas_call(
        functools.partial(_attn_body, lambda_init=lambda_init),
        grid=(heads, bsz, nq),
        in_specs=[pl.BlockSpec((4, A_QK), lambda h, b, qi: (0, 0)),
                  pl.BlockSpec((tstep, LANES), lambda h, b, qi: (b * nq + qi, h)),
                  pl.BlockSpec((seq, LANES), lambda h, b, qi: (b, nb + h)),
                  pl.BlockSpec((seq, LANES), lambda h, b, qi: (b, 2 * nb + h)),
                  pl.BlockSpec((tstep, LANES), lambda h, b, qi: (b * nq + qi, 3 * nb + h)),
                  pl.BlockSpec((1, 8, 2 * tq), lambda h, b, qi: (h, 0, 0)),
                  pl.BlockSpec((1, LANES), lambda h, b, qi: (0, 0))],
        out_specs=pl.BlockSpec((tstep, LANES), lambda h, b, qi: (b * nq + qi, h)),
        out_shape=jax.ShapeDtypeStruct((t, bw), BF16),
        scratch_shapes=[pltpu.VMEM((5, tq, tq), F32), pltpu.VMEM((seq, LANES), BF16),
                        pltpu.VMEM((LANES + BF16_ROWS, seq), BF16),
                        pltpu.VMEM((2, seq // tq, tq, 2 * tq), F32)],
        compiler_params=_cparams(("arbitrary", "arbitrary", "arbitrary")),
        name="diff_attn",
    )(lam_p, proj, proj, proj, proj, band, subln_g.reshape(1, LANES))


def _hgrn_consts(c):
    i = np.arange(c)
    lo = (i[:, None] >= i[None, :]).astype(np.float32)
    levels = int(math.log2(c))
    pmask = np.zeros((2, levels + 1, c, c), np.float32)
    pmask[:, 0] = np.eye(c)
    for l in range(levels):
        m = 1 << l
        same = (i[:, None] // (2 * m)) == (i[None, :] // (2 * m))
        second = (i % (2 * m)) >= m
        pmask[0, l + 1] = same & second[:, None] & ~second[None, :]
        pmask[1, l + 1] = same & ~second[:, None] & second[None, :]
    return (jnp.asarray(lo, BF16), jnp.asarray(lo.T, BF16), jnp.asarray(pmask))


def _boundary(cum, m, rev):
    c = cum.shape[0]
    if 2 * m >= SUBLANES:
        n = c // (2 * m)
        x = cum.reshape(n, 2 * m, LANES)
        r = m if rev else m - 1
        return jnp.broadcast_to(x[:, r:r + 1, :], x.shape).reshape(c, LANES)
    x = cum.reshape(c // SUBLANES, SUBLANES, LANES)
    sub = lax.broadcasted_iota(jnp.int32, x.shape, 1)
    out = None
    for node in range(SUBLANES // (2 * m)):
        r = node * 2 * m + (m if rev else m - 1)
        bc = jnp.broadcast_to(x[:, r:r + 1, :], x.shape)
        out = bc if out is None else jnp.where(sub >= node * 2 * m, bc, out)
    return out.reshape(c, LANES)


def _hgrn_prep(z, q, v, lbp, tri, pmask_ref, rev):
    lb_floor, one_m_lb, lb_pos = lbp
    c = z.shape[0]
    d = 1 if rev else 0
    e = jnp.exp(-jnp.abs(z))
    r = 1.0 / (1.0 + e)
    pos = z >= 0.0
    sig = jnp.where(pos, r, e * r)
    log_sig = jnp.minimum(z, 0.0) - jnp.log(1.0 + e)
    lf = jnp.where(lb_pos, jnp.log(lb_floor + one_m_lb * sig), log_sig)
    k = one_m_lb * jnp.where(pos, e * r, r)
    cum = _split_dot(tri, lf, 1, 3)
    tot = cum[0:1] if rev else cum[c - 1:c]
    vb = v.astype(BF16)
    scores = _dot_nt(q.astype(BF16), k.astype(BF16)) * pmask_ref[d, 0]
    for l in range(int(math.log2(c))):
        e = jnp.exp(-jnp.abs(cum - _boundary(cum, 1 << l, rev)))
        scores = scores + _dot_nt((q * e).astype(BF16), (k * e).astype(BF16)) * pmask_ref[d, l + 1]
        if l == 3:
            yield
    o = _dot(scores.astype(BF16), vb)
    qd = (q * jnp.exp(cum)).astype(BF16)
    kv_t = _dot(v.T.astype(BF16), (k * jnp.exp(tot - cum)).astype(BF16))
    return o, qd, kv_t, jnp.exp(tot)


N_HGRN_IN = 10


def _hgrn_parts(q_ref, v_ref, zf_ref, zb_ref, g_ref, lb_ref, ng_ref, lo_ref, up_ref, pmask_ref,
                o_ref, acc_s):
    seq = q_ref.shape[0]
    c = lo_ref.shape[0]
    n = seq // c
    group = C_GROUP
    lbs = lb_ref[...]

    def lb_params(d):
        lb = lbs[d:d + 1]
        return jnp.maximum(lb, LB_FLOOR), 1.0 - lb, lb > 0.0

    lbp = (lb_params(0), lb_params(1))
    tri = (lo_ref[...], up_ref[...])
    z_ref = (zf_ref, zb_ref)
    acc_s[...] = jnp.zeros_like(acc_s)

    def body(i, carry):
        st = list(carry)
        base = (_aligned(i * (group * c), group * c), _aligned((n // group - 1 - i) * (group * c), group * c))
        prep = []
        for d in range(2):
            for j in range(group):
                rows = pl.ds(base[d] + j * c, c)
                prep.append((yield from _hgrn_prep(z_ref[d][rows, :], q_ref[rows, :], v_ref[rows, :], lbp[d], tri[d],
                                                   pmask_ref, d == 1)))
                yield
        for step in range(group):
            for d in range(2):
                j = step if d == 0 else group - 1 - step
                o, qd, kv_t, dec = prep[d * group + j]
                rows = pl.ds(base[d] + j * c, c)
                acc_s[rows, :] += o + _dot_nt(qd, st[d].astype(BF16))
                st[d] = st[d] * dec + kv_t
            yield
        return tuple(st)

    def finish(_):
        for r0 in range(0, seq, ROW_BLOCK):
            rows = pl.ds(r0, ROW_BLOCK)
            o = acc_s[rows, :]
            ms = jnp.mean(o * o, axis=-1, keepdims=True)
            g = g_ref[rows, :]
            o_ref[rows, :] = (o * lax.rsqrt(ms + NORM_EPS) * ng_ref[...] * (g * _sigmoid(g))).astype(o_ref.dtype)

    zero = jnp.zeros((LANES, LANES), F32)
    return n // group, (zero, zero), body, finish


def _hgrn_operands(proj, lb_l, norm_g, seq, bw):
    nb = bw // LANES
    c0 = 8 * nb + 2
    consts = _hgrn_consts(C_CHUNK)
    col = lambda off: pl.BlockSpec((seq, LANES), lambda b, h: (b, off + h))
    fixed2 = lambda shape: pl.BlockSpec(shape, lambda b, h: (0, 0))
    in_specs = [col(c0), col(c0 + nb), col(c0 + 2 * nb), col(c0 + 3 * nb), col(c0 + 4 * nb),
                pl.BlockSpec((2, LANES), lambda b, h: (0, h)), fixed2((1, LANES)),
                fixed2(consts[0].shape), fixed2(consts[1].shape),
                pl.BlockSpec(consts[2].shape, lambda b, h: (0, 0, 0, 0))]
    operands = (proj, proj, proj, proj, proj, lb_l, norm_g.reshape(1, LANES), *consts)
    assert len(operands) == N_HGRN_IN
    return in_specs, operands, [pltpu.VMEM((seq, LANES), F32)]


def _rwkv_consts(c, group):
    i = np.arange(c)
    j = np.arange(group * c)
    lo = ((j[:, None] >= j[None, :]) & (j[:, None] // c == j[None, :] // c)).astype(np.float32)
    tmask = np.stack([i[:, None] > i[None, :], i[:, None] >= i[None, :],
                      i[:, None] < i[None, :], i[:, None] <= i[None, :]]).astype(np.float32)
    levels = int(math.log2(c))
    lvl = np.zeros((levels, c, c), np.float32)
    for l in range(levels):
        m = 1 << l
        lvl[l] = ((i[:, None] // (2 * m)) == (i[None, :] // (2 * m))) & ((i[:, None] // m) != (i[None, :] // m))
    head = (np.arange(LANES)[:, None] // B_HEAD) == (np.arange(LANES)[None, :] // B_HEAD)
    tile4 = lambda a: np.concatenate([a] * 4, axis=-1)
    quarter = np.stack([np.broadcast_to((np.arange(4 * c) // c) == q, (c, 4 * c)) for q in range(4)])
    return (jnp.asarray(lo, BF16), jnp.asarray(lo.T, BF16), jnp.asarray(tile4(tmask)), jnp.asarray(tile4(lvl)),
            jnp.asarray(head.astype(np.float32), BF16), jnp.asarray(tile4(np.eye(c, dtype=np.float32))),
            jnp.asarray(quarter.astype(np.float32), BF16))


def _shifted(ref, mu, t0, rows, seq):
    x = ref[pl.ds(t0, rows), :]
    row = lax.broadcasted_iota(jnp.int32, x.shape, 0)
    if t0 == 0:
        prev = jnp.where(row == 0, 0.0, pltpu.roll(x, 1, 0))
    else:
        prev = ref[pl.ds(t0 - 1, rows), :]
    if t0 + rows == seq:
        nxt = jnp.where(row == rows - 1, 0.0, pltpu.roll(x, rows - 1, 0))
    else:
        nxt = ref[pl.ds(t0 + 1, rows), :]
    return x + mu[0:1] * (prev - x) + mu[1:2] * (nxt - x)


def _stack_heads(x, lane_lo):
    return jnp.concatenate([jnp.where(lane_lo, x, 0.0), jnp.where(lane_lo, 0.0, x)], axis=0)


def _rwkv_prep(r, lw, kk, kka, v, km, tri, tmask_ref, lvl_ref, eye_ref, quarter_ref, hmask, rev, group):
    c = r.shape[0] // group
    slab = tri.shape[0]
    lane_lo = lax.broadcasted_iota(jnp.int32, (c, LANES), 1) < B_HEAD
    row_lo = lax.broadcasted_iota(jnp.int32, (LANES, c), 0) < B_HEAD
    cw_all = jnp.concatenate([_split_dot(tri, lw[i:i + slab], 1, 3) for i in range(0, group * c, slab)], axis=0)
    strict = tmask_ref[2 if rev else 0]
    incl = tmask_ref[3 if rev else 1]
    eye = eye_ref[...]
    rows = [slice(i * c, (i + 1) * c) for i in range(group)]
    cw = [cw_all[s] for s in rows]
    tot = [x[0:1] if rev else x[c - 1:c] for x in cw]
    e_neg = [jnp.exp(-x) for x in cw]
    e_tot = [jnp.exp(t - x) for t, x in zip(tot, cw)]
    heads = lambda x: _stack_heads(x, lane_lo)
    bd = [kk[s] * jnp.exp(x - lw[s]) for s, x in zip(rows, cw)]
    ad = [-(kka[s] * e) for s, e in zip(rows, e_neg)]
    kd = [km[s] * e for s, e in zip(rows, e_neg)]
    rd = [r[s] * jnp.exp(x) for s, x in zip(rows, cw)]
    ae_t = [(-(kka[s] * e)).T for s, e in zip(rows, e_tot)]
    ke_t = [(km[s] * e).T.astype(BF16) for s, e in zip(rows, e_tot)]
    v2 = [heads(v[s]).astype(BF16) for s in rows]
    dcol = [jnp.broadcast_to(jnp.exp(t), (LANES, LANES)).T for t in tot]
    kv = [_dot(k, v[s].astype(BF16)) * hmask for k, s in zip(ke_t, rows)]
    yield
    pairs = [(2 * i, 2 * i + 1) for i in range(group // 2)]
    side = lambda xs: [jnp.concatenate([xs[a], xs[b]], axis=1) for a, b in pairs]
    lhs = side([jnp.concatenate([b, q], axis=0).astype(BF16) for b, q in zip(bd, rd)])
    rhs = [jnp.concatenate([heads(ad[ch]), heads(kd[ch])], axis=0).astype(BF16) for ch in range(group)]
    gram = [_dot_nt(x, _block_diag(rhs[a], rhs[b])) for x, (a, b) in zip(lhs, pairs)]
    pick = lambda g, i, j: jnp.concatenate([g[i * c:(i + 1) * c, j * LANES:(j + 1) * LANES],
                                            g[i * c:(i + 1) * c, (2 + j) * LANES:(3 + j) * LANES]], axis=1)
    ba = [pick(g, 0, 0) * strict for g in gram]
    bk = [pick(g, 0, 1) * strict for g in gram]
    ra = [pick(g, 1, 0) * incl for g in gram]
    rk = [pick(g, 1, 1) * incl for g in gram]
    yield
    spread = lambda x: jnp.concatenate([x * quarter_ref[q] for q in range(4)], axis=0)
    d = [eye + x * lvl_ref[0] for x in ba]
    for l in range(1, int(math.log2(c))):
        lv = lvl_ref[l]
        db = [x.astype(BF16) for x in d]
        dm = [_dot(x, spread((a * lv).astype(BF16))).astype(BF16) for x, a in zip(db, ba)]
        d = [x + _dot(y, spread(z)) for x, y, z in zip(d, dm, db)]
        yield
    xv = [_dot(jnp.concatenate([x, y], axis=0).astype(BF16), _block_diag(v2[a], v2[b]))
          for x, y, (a, b) in zip(bk, rk, pairs)]
    yield
    lane_lo2 = jnp.concatenate([lane_lo, lane_lo], axis=1)
    heads2_wide = lambda z: jnp.concatenate([jnp.where(lane_lo2, z, 0.0), jnp.where(lane_lo2, 0.0, z)], axis=0)
    zrows = lambda y, ch: heads2_wide(jnp.concatenate([y, bd[ch]], axis=1)).astype(BF16)
    tz = [_dot(x.astype(BF16), _block_diag(zrows(y[:c, :LANES], a), zrows(y[:c, LANES:], b)))
          for x, y, (a, b) in zip(d, xv, pairs)]
    tr, rae, add = [], [], []
    for i, (a, b) in enumerate(pairs):
        for h, ch in enumerate((a, b)):
            t = tz[i][:, 2 * h * LANES:(2 * h + 2) * LANES]
            y = xv[i][:, h * LANES:(h + 1) * LANES]
            aet = jnp.concatenate([jnp.where(row_lo, ae_t[ch], 0.0), jnp.where(row_lo, 0.0, ae_t[ch])], axis=1)
            tr.append(jnp.concatenate([t[:, LANES:], rd[ch]], axis=0).astype(BF16))
            rae.append(jnp.concatenate([ra[i][:, h * LANES:(h + 1) * LANES], aet], axis=0).astype(BF16))
            add.append(jnp.concatenate([t[:, :LANES], y[c:], kv[ch], dcol[ch]], axis=0))
    return tr, rae, add


def _block_diag(x, y):
    z = jnp.zeros_like(x)
    return jnp.concatenate([jnp.concatenate([x, z], axis=1), jnp.concatenate([z, y], axis=1)], axis=0)


def _rwkv_step(tr, rae, add, h):
    c = tr[0].shape[0] // 2
    lane_lo = lax.broadcasted_iota(jnp.int32, (c, LANES), 1) < B_HEAD
    side = lambda f, b: jnp.concatenate([f, b], axis=1)
    lanes = lambda x, d: x[:, d * LANES:(d + 1) * LANES]
    hb = h.astype(BF16)
    p = _dot(side(*tr), _block_diag(hb[:, :LANES], hb[:, LANES:]))
    u = [_stack_heads(add[d][0:c] + lanes(p[:c], d), lane_lo).astype(BF16) for d in range(2)]
    qq = _dot(side(*rae), _block_diag(*u))
    ys = [lanes(p[c:], d) + lanes(qq[:c], d) + add[d][c:2 * c] for d in range(2)]
    kv, dec = slice(2 * c, 2 * c + LANES), slice(2 * c + LANES, 2 * c + 2 * LANES)
    h = side(add[0][dec], add[1][dec]) * h + qq[c:] + side(add[0][kv], add[1][kv])
    return ys, h


N_RWKV_IN = 27
N_RWKV_SCRATCH = 11


def _rwkv_parts(r_ref, k_ref, v_ref, wd_ref, ad_ref, g_ref, mur_ref, muk_ref, muv_ref, muw_ref, mua_ref,
                w0_ref, a0_ref, wup_ref, aup_ref, kk_ref, ka_ref, rk_ref, lng_ref, lnb_ref,
                lo_ref, up_ref, tmask_ref, lvl_ref, head_ref, eye_ref, quarter_ref,
                o_ref, r_s, v_s, kk_s, lw_s, kka_s, km_s, bv_s, y_s, tr_s, rae_s, add_s):
    seq = r_ref.shape[0]
    c = B_CHUNK
    n = seq // c
    rb = ROW_BLOCK
    group = B_GROUP
    gc = group * c
    n_it = n // group
    head = head_ref[...]
    hmask = head.astype(F32)
    lane_lo = lax.broadcasted_iota(jnp.int32, (rb, LANES), 1) < B_LORA

    def prologue(t0):
        rows = pl.ds(t0, rb)
        r = _shifted(r_ref, mur_ref[...], t0, rb, seq)
        k = _shifted(k_ref, muk_ref[...], t0, rb, seq)
        v = _shifted(v_ref, muv_ref[...], t0, rb, seq)
        yield
        twd = jnp.tanh(_shifted(wd_ref, muw_ref[...], t0, rb, seq))
        ad = _shifted(ad_ref, mua_ref[...], t0, rb, seq)
        kk = k * kk_ref[...]
        kk = kk / jnp.maximum(jnp.sqrt(_split_dot(kk * kk, head, 0, 2)), 1e-12)
        kmsum = None
        for d in range(2):
            yield
            sel = lane_lo if d == 0 else jnp.logical_not(lane_lo)
            wl = _dot(jnp.where(sel, twd, 0.0).astype(BF16), wup_ref[...].astype(BF16))
            al = _dot(jnp.where(sel, ad, 0.0).astype(BF16), aup_ref[...].astype(BF16))
            lw_s[d, rows, :] = -math.exp(-0.5) * _sigmoid(w0_ref[d:d + 1, :] + wl)
            a = _sigmoid(a0_ref[d:d + 1, :] + al)
            km = k * (1.0 + (a - 1.0) * ka_ref[...])
            kka_s[d, rows, :] = kk * a
            km_s[d, rows, :] = km
            kmsum = km if kmsum is None else kmsum + km
        bonus = _split_dot(r * kmsum * rk_ref[...], head, 0, 2)
        r_s[rows, :] = r
        v_s[rows, :] = v
        kk_s[rows, :] = kk
        bv_s[rows, :] = bonus * v
        yield

    first = [t0 for t0 in range(0, seq, rb) if t0 < gc or t0 >= seq - gc]
    later = [t0 for t0 in range(0, seq, rb) if t0 not in first]
    for t0 in first:
        _interleave(prologue(t0))

    def prologue_rest():
        for t0 in later:
            yield from prologue(t0)

    y_s[...] = jnp.zeros_like(y_s)
    tri_lo, tri_up = lo_ref[...], up_ref[...]

    def bases(i):
        return _aligned(i * gc, gc), _aligned((n_it - 1 - i) * gc, gc)

    def prep(d, base, tri):
        rows = pl.ds(base, gc)
        tr, rae, add = yield from _rwkv_prep(
            r_s[rows, :], lw_s[d, rows, :], kk_s[rows, :], kka_s[d, rows, :], v_s[rows, :], km_s[d, rows, :],
            tri, tmask_ref, lvl_ref, eye_ref, quarter_ref, hmask, d == 1, group)
        for j in range(group):
            tr_s[d, j] = tr[j]
            rae_s[d, j] = rae[j]
            add_s[d, j] = add[j]

    def body(i, h):
        base_f, base_b = bases(i)
        yield from _lockstep(prep(0, base_f, tri_lo), prep(1, base_b, tri_up))
        for j in range(group):
            jb = group - 1 - j
            (yf, yb), h = _rwkv_step((tr_s[0, j], tr_s[1, jb]), (rae_s[0, j], rae_s[1, jb]),
                                     (add_s[0, j], add_s[1, jb]), h)
            y_s[pl.ds(base_f + j * c, c), :] += yf
            y_s[pl.ds(base_b + jb * c, c), :] += yb
            yield
        return h

    def finish(_):
        inv_n = 1.0 / B_HEAD
        for t0 in range(0, seq, rb):
            rows = pl.ds(t0, rb)
            y = y_s[rows, :]
            mu = _split_dot(y, head, 0, 2) * inv_n
            yc = y - mu
            var = _split_dot(yc * yc, head, 0, 2) * inv_n
            yn = yc * lax.rsqrt(var + B_LNX_EPS) * lng_ref[...] + lnb_ref[...]
            g = g_ref[rows, :]
            o_ref[rows, :] = ((yn + bv_s[rows, :]) * (g * _sigmoid(g))).astype(o_ref.dtype)

    return n_it, jnp.zeros((LANES, 2 * LANES), F32), body, finish, prologue_rest()


def _rwkv_operands(proj, mu, w0, a0, wup, aup, k_k, k_a, r_k, lnx_g, lnx_b, seq, bw):
    nb = bw // LANES
    c0 = 4 * nb
    consts = _rwkv_consts(B_CHUNK, B_CUM)
    col = lambda off: pl.BlockSpec((seq, LANES), lambda b, g: (b, off + g))
    colf = lambda off: pl.BlockSpec((seq, LANES), lambda b, g: (b, off))
    par2 = lambda off: pl.BlockSpec((2, LANES), lambda b, g: (0, off + g))
    par2f = lambda off: pl.BlockSpec((2, LANES), lambda b, g: (0, off))
    par1 = pl.BlockSpec((1, LANES), lambda b, g: (0, g))
    up = pl.BlockSpec((2 * B_LORA, LANES), lambda b, g: (0, g))

    def fixed(a):
        return pl.BlockSpec(a.shape, lambda b, g: (0,) * a.ndim)

    row = lambda a: a.reshape(1, bw)
    in_specs = [col(c0), col(c0 + nb), col(c0 + 2 * nb), colf(c0 + 3 * nb), colf(c0 + 3 * nb + 1), col(c0 + 3 * nb + 2),
                par2(0), par2(nb), par2(2 * nb), par2f(3 * nb), par2f(3 * nb + 1),
                par2(0), par2(0), up, up, par1, par1, par1, par1, par1] + [fixed(a) for a in consts]
    operands = (proj, proj, proj, proj, proj, proj, mu, mu, mu, mu, mu, w0, a0,
                wup.reshape(2 * B_LORA, bw), aup.reshape(2 * B_LORA, bw),
                row(k_k), row(k_a), row(r_k), row(lnx_g), row(lnx_b), *consts)
    scratch = [pltpu.VMEM((seq, LANES), F32), pltpu.VMEM((seq, LANES), F32), pltpu.VMEM((seq, LANES), F32),
               pltpu.VMEM((2, seq, LANES), F32), pltpu.VMEM((2, seq, LANES), F32),
               pltpu.VMEM((2, seq, LANES), F32), pltpu.VMEM((seq, LANES), F32),
               pltpu.VMEM((seq, LANES), F32),
               pltpu.VMEM((2, B_GROUP, 2 * B_CHUNK, LANES), BF16),
               pltpu.VMEM((2, B_GROUP, B_CHUNK + LANES, LANES), BF16),
               pltpu.VMEM((2, B_GROUP, 2 * B_CHUNK + 2 * LANES, LANES), F32)]
    assert len(operands) == N_RWKV_IN and len(scratch) == N_RWKV_SCRATCH
    return in_specs, operands, scratch


def _recurrent_body(*refs):
    rwkv_in = refs[:N_RWKV_IN]
    hgrn_in = refs[N_RWKV_IN:N_RWKV_IN + N_HGRN_IN]
    rwkv_out, hgrn_out = refs[N_RWKV_IN + N_HGRN_IN:N_RWKV_IN + N_HGRN_IN + 2]
    scratch = refs[N_RWKV_IN + N_HGRN_IN + 2:]
    trips_b, carry_b, body_b, finish_b, prologue_rest = _rwkv_parts(*rwkv_in, rwkv_out, *scratch[:N_RWKV_SCRATCH])
    trips_c, carry_c, body_c, finish_c = _hgrn_parts(*hgrn_in, hgrn_out, *scratch[N_RWKV_SCRATCH:])
    assert trips_b == trips_c and trips_b >= 2

    def body(i, carry):
        return tuple(_interleave(body_b(i, carry[0]), body_c(i, carry[1])))

    carry_b, carry_c, _ = _interleave(body_b(0, carry_b), body_c(0, carry_c), prologue_rest)
    carry_b, carry_c = lax.fori_loop(1, trips_b, body, (carry_b, carry_c))
    finish_b(carry_b)
    finish_c(carry_c)


def _recurrent(proj, rwkv_params, hgrn_params, bsz, seq, bw):
    t = proj.shape[0]
    nb = bw // LANES
    specs_b, ops_b, scratch_b = _rwkv_operands(proj, *rwkv_params, seq, bw)
    specs_c, ops_c, scratch_c = _hgrn_operands(proj, *hgrn_params, seq, bw)
    out_spec = pl.BlockSpec((seq, LANES), lambda b, g: (b, g))
    out_shape = jax.ShapeDtypeStruct((t, bw), BF16)
    return pl.pallas_call(
        _recurrent_body,
        grid=(bsz, nb),
        in_specs=specs_b + specs_c,
        out_specs=[out_spec, out_spec],
        out_shape=[out_shape, out_shape],
        scratch_shapes=scratch_b + scratch_c,
        compiler_params=_cparams(("parallel", "parallel")),
        name="rwkv7_hgrn2",
    )(*ops_b, *ops_c)


def kernel(x, rel_bias, pre_norm_g, post_norm_g, w_in, w_out, lambda_q1, lambda_k1, lambda_q2, lambda_k2, subln_g, rwkv_shift_mu, rwkv_w0, rwkv_w_up, rwkv_a0, rwkv_a_up, rwkv_k_k, rwkv_k_a, rwkv_r_k, rwkv_lnx_g, rwkv_lnx_b, hgrn_lb_logits, hgrn_norm_g):
    bsz, seq, d = x.shape
    depth = w_in.shape[0]
    bw = d // 2
    assert bw % (2 * LANES) == 0 and seq % (ATT_SUB * ATT_BLOCK) == 0 and seq % (C_GROUP * C_CHUNK) == 0
    assert seq % (B_GROUP * B_CHUNK) == 0
    assert w_in.shape[2] == 13 * bw + 4 * B_LORA
    x2 = x.reshape(bsz * seq, d)
    lb = jax.nn.softmax(hgrn_lb_logits.astype(F32), axis=1)
    lb = jnp.cumsum(lb, axis=1) - lb[:, :1]
    band = _bias_band(rel_bias, ATT_BLOCK)
    w_in_b, w_out_b = w_in.astype(BF16), w_out.astype(BF16)
    for l in range(depth):
        proj = _proj(x2, pre_norm_g[l], w_in_b, l)
        lam_p = jnp.stack([lambda_q1[l], lambda_k1[l], lambda_q2[l], lambda_k2[l]]).astype(F32)
        ya = _attn(proj, lam_p, band, subln_g[l], bsz, seq, bw, l)
        yb, yc = _recurrent(proj, (rwkv_shift_mu[l], rwkv_w0[l], rwkv_a0[l], rwkv_w_up[l], rwkv_a_up[l], rwkv_k_k[l],
                                   rwkv_k_a[l], rwkv_r_k[l], rwkv_lnx_g[l], rwkv_lnx_b[l]),
                            (lb[:, l], hgrn_norm_g[l]), bsz, seq, bw)
        x2 = _out(ya, yb, yc, w_out_b, l, post_norm_g[l], x2)
    return x2.reshape(bsz, seq, d)
```

```python
import functools
import math

import numpy as np
import jax
import jax.numpy as jnp
from jax import lax
from jax.experimental import pallas as pl
from jax.experimental.pallas import tpu as pltpu

F32 = jnp.float32
BF16 = jnp.bfloat16

LANES = 128
SUBLANES = 8
BF16_ROWS = 16
VMEM_LIMIT = 55 * 1024 * 1024

LOG2E = math.log2(math.e)

NORM_EPS = 1e-6
A_QK = 64
A_SUBLN_EPS = 1e-5
NUM_BUCKETS = 32
MAX_DISTANCE = 128
ATT_BLOCK = 256
ATT_SUB = 8
B_HEAD = 64
B_LORA = 64
B_LNX_EPS = 64e-5
B_CHUNK = 64
B_GROUP = 8
B_CUM = 4
C_CHUNK = 128
C_GROUP = 4
LB_FLOOR = 1e-30
ROW_BLOCK = 256

_NT = (((1,), (1,)), ((), ()))


def _dot(a, b):
    return jnp.dot(a, b, preferred_element_type=F32)


def _dot_nt(a, b):
    return lax.dot_general(a, b, _NT, preferred_element_type=F32)


def _split_dot(a, b, split, passes):
    rem = a if split == 0 else b
    acc = None
    for _ in range(passes):
        piece = rem.astype(BF16)
        d = _dot(piece, b) if split == 0 else _dot(a, piece)
        acc = d if acc is None else acc + d
        rem = rem - piece.astype(F32)
    return acc


def _sigmoid(x):
    return 1.0 / (1.0 + jnp.exp(-x))


def _lockstep(*gens):
    results = [None] * len(gens)
    live = list(range(len(gens)))
    while live:
        for k in list(live):
            try:
                next(gens[k])
            except StopIteration as stop:
                results[k] = stop.value
                live.remove(k)
        yield
    return results


def _interleave(*gens):
    stepper = _lockstep(*gens)
    while True:
        try:
            next(stepper)
        except StopIteration as stop:
            return stop.value


def _aligned(x, m):
    return x if isinstance(x, int) else pl.multiple_of(x, m)


def _cparams(sem):
    return pltpu.CompilerParams(dimension_semantics=sem, vmem_limit_bytes=VMEM_LIMIT)


def _proj_body(x_ref, g_ref, w_ref, o_ref, u_s):
    @pl.when(pl.program_id(1) == 0)
    def _():
        xf = x_ref[...]
        ms = jnp.mean(xf * xf, axis=-1, keepdims=True)
        u_s[...] = (xf * lax.rsqrt(ms + NORM_EPS) * g_ref[...]).astype(BF16)

    o_ref[...] = _dot(u_s[...], w_ref[...])


def _proj(x2, g, w, layer, tm=1024, tn=1024):
    t, d = x2.shape
    p = w.shape[2]
    return pl.pallas_call(
        _proj_body,
        grid=(t // tm, pl.cdiv(p, tn)),
        in_specs=[pl.BlockSpec((tm, d), lambda i, j: (i, 0)),
                  pl.BlockSpec((1, d), lambda i, j: (0, 0)),
                  pl.BlockSpec((None, d, tn), lambda i, j: (layer, 0, j))],
        out_specs=pl.BlockSpec((tm, tn), lambda i, j: (i, j)),
        out_shape=jax.ShapeDtypeStruct((t, p), F32),
        scratch_shapes=[pltpu.VMEM((tm, d), BF16)],
        compiler_params=_cparams(("parallel", "arbitrary")),
        name="proj_in",
    )(x2, g.reshape(1, d), w)


def _out_body(ya_ref, yb_ref, yc_ref, w_ref, g_ref, x_ref, o_ref):
    bw = ya_ref.shape[1]
    m = (_dot(ya_ref[...], w_ref[0:bw, :]) + _dot(yb_ref[...], w_ref[bw:2 * bw, :])
         + _dot(yc_ref[...], w_ref[2 * bw:3 * bw, :]))
    ms = jnp.mean(m * m, axis=-1, keepdims=True)
    o_ref[...] = x_ref[...] + m * lax.rsqrt(ms + NORM_EPS) * g_ref[...]


def _out(ya, yb, yc, w, layer, g, x2, tm=256):
    t, d = x2.shape
    bw = ya.shape[1]
    row = lambda i: (i, 0)
    fixed = lambda i: (0, 0)
    return pl.pallas_call(
        _out_body,
        grid=(t // tm,),
        in_specs=[pl.BlockSpec((tm, bw), row), pl.BlockSpec((tm, bw), row), pl.BlockSpec((tm, bw), row),
                  pl.BlockSpec((None, 3 * bw, d), lambda i: (layer, 0, 0)), pl.BlockSpec((1, d), fixed),
                  pl.BlockSpec((tm, d), row)],
        out_specs=pl.BlockSpec((tm, d), row),
        out_shape=jax.ShapeDtypeStruct((t, d), F32),
        compiler_params=_cparams(("parallel",)),
        name="proj_out",
    )(ya, yb, yc, w, g.reshape(1, d), x2)


def _t5_bucket(rel):
    half = NUM_BUCKETS // 2
    max_exact = half // 2
    n = jnp.abs(rel)
    nf = jnp.maximum(n, max_exact).astype(F32)
    large = max_exact + (jnp.log(nf / max_exact) / math.log(MAX_DISTANCE / max_exact)
                         * (half - max_exact)).astype(jnp.int32)
    large = jnp.minimum(large, half - 1)
    return jnp.where(rel > 0, half, 0) + jnp.where(n < max_exact, n, large)


def _bias_band(rel_bias, blk):
    c = blk - jnp.arange(2 * blk, dtype=jnp.int32)
    far = jnp.full((2 * blk,), 2 * blk, jnp.int32)
    rel = jnp.stack([c, c + blk, c - blk, -far, far, far, far, far])
    return jnp.transpose(rel_bias.astype(F32)[_t5_bucket(rel)], (2, 0, 1))


def _attn_body(lam_ref, q_ref, k_ref, v_ref, g_ref, band_ref, sg_ref, o_ref, tile_s, kb_s, vt_s, s_s, *,
               lambda_init):
    b = pl.program_id(1)
    qi = pl.program_id(2)
    tq = tile_s.shape[1]
    nsub = q_ref.shape[0] // tq
    nk = k_ref.shape[0] // tq

    @pl.when((b == 0) & (qi == 0))
    def _():
        vec = band_ref[0] * LOG2E
        for d in range(3):
            w = jnp.broadcast_to(vec[d:d + 1, :], (tq, 2 * tq))
            tile_s[d] = pltpu.roll(w, tq, 1, stride=1, stride_axis=0)[:, :tq]
        for d in range(3, 5):
            tile_s[d] = jnp.broadcast_to(vec[d:d + 1, :tq], (tq, tq))

    @pl.when(qi == 0)
    def _():
        row = lax.broadcasted_iota(jnp.int32, (BF16_ROWS, k_ref.shape[0]), 0)
        vt_s[LANES:LANES + BF16_ROWS, :] = jnp.where(row == 0, 1.0, 0.0).astype(BF16)
        for kj in range(nk):
            rows = pl.ds(kj * tq, tq)
            kb_s[rows, :] = k_ref[rows, :].astype(BF16)
            vt_s[0:LANES, kj * tq:(kj + 1) * tq] = v_ref[rows, :].T.astype(BF16)

    lp = lam_ref[...]
    lam = (jnp.exp(jnp.sum(lp[0:1] * lp[1:2], axis=-1, keepdims=True))
           - jnp.exp(jnp.sum(lp[2:3] * lp[3:4], axis=-1, keepdims=True)) + lambda_init)

    lane = lax.broadcasted_iota(jnp.int32, (tq, LANES), 1)

    def pass1(sub):
        q = q_ref[pl.ds(sub * tq, tq), :] * (A_QK ** -0.5 * LOG2E)
        q2 = jnp.concatenate([jnp.where(lane < A_QK, q, 0.0), jnp.where(lane >= A_QK, q, 0.0)], axis=0).astype(BF16)
        mx = None
        for kj in range(nk):
            d = kj - (qi * nsub + sub)
            bias = tile_s[jnp.where(d == 0, 0, jnp.where(d == 1, 1, jnp.where(d == -1, 2, jnp.where(d > 0, 4, 3))))]
            s = _dot_nt(kb_s[pl.ds(kj * tq, tq), :], q2) + jnp.concatenate([bias, bias], axis=1)
            s_s[sub % 2, kj] = s
            blk_max = jnp.max(s.reshape(tq // SUBLANES, SUBLANES, 2 * tq), axis=0)
            mx = blk_max if mx is None else jnp.maximum(mx, blk_max)
            yield
        return jnp.max(mx, axis=0, keepdims=True)

    def pass2(sub, m):
        acc = None
        for kj in range(nk):
            p = jnp.exp2(s_s[sub % 2, kj] - m).astype(BF16)
            pv = _dot(vt_s[:, kj * tq:(kj + 1) * tq], p)
            acc = pv if acc is None else acc + pv
            yield
        out = acc[0:LANES] / acc[LANES:LANES + 1]
        att = (out[:, :tq] - lam * out[:, tq:]).T
        ms = jnp.mean(att * att, axis=-1, keepdims=True)
        y = att * lax.rsqrt(ms + A_SUBLN_EPS) * sg_ref[...] * (1.0 - lambda_init)
        g = g_ref[pl.ds(sub * tq, tq), :]
        o_ref[pl.ds(sub * tq, tq), :] = (y * (g * _sigmoid(g))).astype(o_ref.dtype)

    m = _interleave(pass1(0))[0]
    for sub in range(nsub):
        if sub + 1 < nsub:
            _, m = _interleave(pass2(sub, m), pass1(sub + 1))
        else:
            _interleave(pass2(sub, m))


def _attn(proj, lam_p, band, subln_g, bsz, seq, bw, layer_idx):
    t = proj.shape[0]
    nb = bw // LANES
    heads = bw // (2 * A_QK)
    tq = ATT_BLOCK
    tstep = ATT_SUB * tq
    nq = seq // tstep
    lambda_init = 0.8 - 0.6 * math.exp(-0.3 * layer_idx)
    return pl.pallas_call(
        functools.partial(_attn_body, lambda_init=lambda_init),
        grid=(heads, bsz, nq),
        in_specs=[pl.BlockSpec((4, A_QK), lambda h, b, qi: (0, 0)),
                  pl.BlockSpec((tstep, LANES), lambda h, b, qi: (b * nq + qi, h)),
                  pl.BlockSpec((seq, LANES), lambda h, b, qi: (b, nb + h)),
                  pl.BlockSpec((seq, LANES), lambda h, b, qi: (b, 2 * nb + h)),
                  pl.BlockSpec((tstep, LANES), lambda h, b, qi: (b * nq + qi, 3 * nb + h)),
                  pl.BlockSpec((1, 8, 2 * tq), lambda h, b, qi: (h, 0, 0)),
                  pl.BlockSpec((1, LANES), lambda h, b, qi: (0, 0))],
        out_specs=pl.BlockSpec((tstep, LANES), lambda h, b, qi: (b * nq + qi, h)),
        out_shape=jax.ShapeDtypeStruct((t, bw), BF16),
        scratch_shapes=[pltpu.VMEM((5, tq, tq), F32), pltpu.VMEM((seq, LANES), BF16),
                        pltpu.VMEM((LANES + BF16_ROWS, seq), BF16),
                        pltpu.VMEM((2, seq // tq, tq, 2 * tq), F32)],
        compiler_params=_cparams(("arbitrary", "arbitrary", "arbitrary")),
        name="diff_attn",
    )(lam_p, proj, proj, proj, proj, band, subln_g.reshape(1, LANES))


def _hgrn_consts(c):
    i = np.arange(c)
    lo = (i[:, None] >= i[None, :]).astype(np.float32)
    levels = int(math.log2(c))
    pmask = np.zeros((2, levels + 1, c, c), np.float32)
    pmask[:, 0] = np.eye(c)
    for l in range(levels):
        m = 1 << l
        same = (i[:, None] // (2 * m)) == (i[None, :] // (2 * m))
        second = (i % (2 * m)) >= m
        pmask[0, l + 1] = same & second[:, None] & ~second[None, :]
        pmask[1, l + 1] = same & ~second[:, None] & second[None, :]
    return (jnp.asarray(lo, BF16), jnp.asarray(lo.T, BF16), jnp.asarray(pmask))


def _boundary(cum, m, rev):
    c = cum.shape[0]
    if 2 * m >= SUBLANES:
        n = c // (2 * m)
        x = cum.reshape(n, 2 * m, LANES)
        r = m if rev else m - 1
        return jnp.broadcast_to(x[:, r:r + 1, :], x.shape).reshape(c, LANES)
    x = cum.reshape(c // SUBLANES, SUBLANES, LANES)
    sub = lax.broadcasted_iota(jnp.int32, x.shape, 1)
    out = None
    for node in range(SUBLANES // (2 * m)):
        r = node * 2 * m + (m if rev else m - 1)
        bc = jnp.broadcast_to(x[:, r:r + 1, :], x.shape)
        out = bc if out is None else jnp.where(sub >= node * 2 * m, bc, out)
    return out.reshape(c, LANES)


def _hgrn_prep(z, q, v, lbp, tri, pmask_ref, rev):
    lb_floor, one_m_lb, lb_pos = lbp
    c = z.shape[0]
    d = 1 if rev else 0
    e = jnp.exp(-jnp.abs(z))
    r = 1.0 / (1.0 + e)
    pos = z >= 0.0
    sig = jnp.where(pos, r, e * r)
    log_sig = jnp.minimum(z, 0.0) - jnp.log(1.0 + e)
    lf = jnp.where(lb_pos, jnp.log(lb_floor + one_m_lb * sig), log_sig)
    k = one_m_lb * jnp.where(pos, e * r, r)
    cum = _split_dot(tri, lf, 1, 3)
    tot = cum[0:1] if rev else cum[c - 1:c]
    vb = v.astype(BF16)
    scores = _dot_nt(q.astype(BF16), k.astype(BF16)) * pmask_ref[d, 0]
    for l in range(int(math.log2(c))):
        e = jnp.exp(-jnp.abs(cum - _boundary(cum, 1 << l, rev)))
        scores = scores + _dot_nt((q * e).astype(BF16), (k * e).astype(BF16)) * pmask_ref[d, l + 1]
        if l == 3:
            yield
    o = _dot(scores.astype(BF16), vb)
    qd = (q * jnp.exp(cum)).astype(BF16)
    kv_t = _dot(v.T.astype(BF16), (k * jnp.exp(tot - cum)).astype(BF16))
    return o, qd, kv_t, jnp.exp(tot)


N_HGRN_IN = 10


def _hgrn_parts(q_ref, v_ref, zf_ref, zb_ref, g_ref, lb_ref, ng_ref, lo_ref, up_ref, pmask_ref,
                o_ref, acc_s):
    seq = q_ref.shape[0]
    c = lo_ref.shape[0]
    n = seq // c
    group = C_GROUP
    lbs = lb_ref[...]

    def lb_params(d):
        lb = lbs[d:d + 1]
        return jnp.maximum(lb, LB_FLOOR), 1.0 - lb, lb > 0.0

    lbp = (lb_params(0), lb_params(1))
    tri = (lo_ref[...], up_ref[...])
    z_ref = (zf_ref, zb_ref)
    acc_s[...] = jnp.zeros_like(acc_s)

    def body(i, carry):
        st = list(carry)
        base = (_aligned(i * (group * c), group * c), _aligned((n // group - 1 - i) * (group * c), group * c))
        prep = []
        for d in range(2):
            for j in range(group):
                rows = pl.ds(base[d] + j * c, c)
                prep.append((yield from _hgrn_prep(z_ref[d][rows, :], q_ref[rows, :], v_ref[rows, :], lbp[d], tri[d],
                                                   pmask_ref, d == 1)))
                yield
        for step in range(group):
            for d in range(2):
                j = step if d == 0 else group - 1 - step
                o, qd, kv_t, dec = prep[d * group + j]
                rows = pl.ds(base[d] + j * c, c)
                acc_s[rows, :] += o + _dot_nt(qd, st[d].astype(BF16))
                st[d] = st[d] * dec + kv_t
            yield
        return tuple(st)

    def finish(_):
        for r0 in range(0, seq, ROW_BLOCK):
            rows = pl.ds(r0, ROW_BLOCK)
            o = acc_s[rows, :]
            ms = jnp.mean(o * o, axis=-1, keepdims=True)
            g = g_ref[rows, :]
            o_ref[rows, :] = (o * lax.rsqrt(ms + NORM_EPS) * ng_ref[...] * (g * _sigmoid(g))).astype(o_ref.dtype)

    zero = jnp.zeros((LANES, LANES), F32)
    return n // group, (zero, zero), body, finish


def _hgrn_operands(proj, lb_l, norm_g, seq, bw):
    nb = bw // LANES
    c0 = 8 * nb + 2
    consts = _hgrn_consts(C_CHUNK)
    col = lambda off: pl.BlockSpec((seq, LANES), lambda b, h: (b, off + h))
    fixed2 = lambda shape: pl.BlockSpec(shape, lambda b, h: (0, 0))
    in_specs = [col(c0), col(c0 + nb), col(c0 + 2 * nb), col(c0 + 3 * nb), col(c0 + 4 * nb),
                pl.BlockSpec((2, LANES), lambda b, h: (0, h)), fixed2((1, LANES)),
                fixed2(consts[0].shape), fixed2(consts[1].shape),
                pl.BlockSpec(consts[2].shape, lambda b, h: (0, 0, 0, 0))]
    operands = (proj, proj, proj, proj, proj, lb_l, norm_g.reshape(1, LANES), *consts)
    assert len(operands) == N_HGRN_IN
    return in_specs, operands, [pltpu.VMEM((seq, LANES), F32)]


def _rwkv_consts(c, group):
    i = np.arange(c)
    j = np.arange(group * c)
    lo = ((j[:, None] >= j[None, :]) & (j[:, None] // c == j[None, :] // c)).astype(np.float32)
    tmask = np.stack([i[:, None] > i[None, :], i[:, None] >= i[None, :],
                      i[:, None] < i[None, :], i[:, None] <= i[None, :]]).astype(np.float32)
    levels = int(math.log2(c))
    lvl = np.zeros((levels, c, c), np.float32)
    for l in range(levels):
        m = 1 << l
        lvl[l] = ((i[:, None] // (2 * m)) == (i[None, :] // (2 * m))) & ((i[:, None] // m) != (i[None, :] // m))
    head = (np.arange(LANES)[:, None] // B_HEAD) == (np.arange(LANES)[None, :] // B_HEAD)
    tile4 = lambda a: np.concatenate([a] * 4, axis=-1)
    quarter = np.stack([np.broadcast_to((np.arange(4 * c) // c) == q, (c, 4 * c)) for q in range(4)])
    return (jnp.asarray(lo, BF16), jnp.asarray(lo.T, BF16), jnp.asarray(tile4(tmask)), jnp.asarray(tile4(lvl)),
            jnp.asarray(head.astype(np.float32), BF16), jnp.asarray(tile4(np.eye(c, dtype=np.float32))),
            jnp.asarray(quarter.astype(np.float32), BF16))


def _shifted(ref, mu, t0, rows, seq):
    x = ref[pl.ds(t0, rows), :]
    row = lax.broadcasted_iota(jnp.int32, x.shape, 0)
    if t0 == 0:
        prev = jnp.where(row == 0, 0.0, pltpu.roll(x, 1, 0))
    else:
        prev = ref[pl.ds(t0 - 1, rows), :]
    if t0 + rows == seq:
        nxt = jnp.where(row == rows - 1, 0.0, pltpu.roll(x, rows - 1, 0))
    else:
        nxt = ref[pl.ds(t0 + 1, rows), :]
    return x + mu[0:1] * (prev - x) + mu[1:2] * (nxt - x)


def _stack_heads(x, lane_lo):
    return jnp.concatenate([jnp.where(lane_lo, x, 0.0), jnp.where(lane_lo, 0.0, x)], axis=0)


def _rwkv_prep(r, lw, cw_all, kk, kka, v, km, tmask_ref, lvl_ref, eye_ref, quarter_ref, hmask, rev, group):
    c = r.shape[0] // group
    lane_lo = lax.broadcasted_iota(jnp.int32, (c, LANES), 1) < B_HEAD
    row_lo = lax.broadcasted_iota(jnp.int32, (LANES, c), 0) < B_HEAD
    strict = tmask_ref[2 if rev else 0]
    incl = tmask_ref[3 if rev else 1]
    eye = eye_ref[...]
    rows = [slice(i * c, (i + 1) * c) for i in range(group)]
    cw = [cw_all[s] for s in rows]
    tot = [x[0:1] if rev else x[c - 1:c] for x in cw]
    e_neg = [jnp.exp(-x) for x in cw]
    e_tot = [jnp.exp(t - x) for t, x in zip(tot, cw)]
    heads = lambda x: _stack_heads(x, lane_lo)
    bd = [kk[s] * jnp.exp(x - lw[s]) for s, x in zip(rows, cw)]
    ad = [-(kka[s] * e) for s, e in zip(rows, e_neg)]
    kd = [km[s] * e for s, e in zip(rows, e_neg)]
    rd = [r[s] * jnp.exp(x) for s, x in zip(rows, cw)]
    ae_t = [(-(kka[s] * e)).T for s, e in zip(rows, e_tot)]
    ke_t = [(km[s] * e).T.astype(BF16) for s, e in zip(rows, e_tot)]
    v2 = [heads(v[s]).astype(BF16) for s in rows]
    dcol = [jnp.broadcast_to(jnp.exp(t), (LANES, LANES)).T for t in tot]
    kv = [_dot(k, v[s].astype(BF16)) * hmask for k, s in zip(ke_t, rows)]
    yield
    pairs = [(2 * i, 2 * i + 1) for i in range(group // 2)]
    side = lambda xs: [jnp.concatenate([xs[a], xs[b]], axis=1) for a, b in pairs]
    lhs = side([jnp.concatenate([b, q], axis=0).astype(BF16) for b, q in zip(bd, rd)])
    rhs = [jnp.concatenate([heads(ad[ch]), heads(kd[ch])], axis=0).astype(BF16) for ch in range(group)]
    gram = [_dot_nt(x, _block_diag(rhs[a], rhs[b])) for x, (a, b) in zip(lhs, pairs)]
    pick = lambda g, i, j: jnp.concatenate([g[i * c:(i + 1) * c, j * LANES:(j + 1) * LANES],
                                            g[i * c:(i + 1) * c, (2 + j) * LANES:(3 + j) * LANES]], axis=1)
    ba = [pick(g, 0, 0) * strict for g in gram]
    bk = [pick(g, 0, 1) * strict for g in gram]
    ra = [pick(g, 1, 0) * incl for g in gram]
    rk = [pick(g, 1, 1) * incl for g in gram]
    yield
    spread = lambda x: jnp.concatenate([x * quarter_ref[q] for q in range(4)], axis=0)
    d = [eye + x * lvl_ref[0] for x in ba]
    for l in range(1, int(math.log2(c))):
        lv = lvl_ref[l]
        db = [x.astype(BF16) for x in d]
        dm = [_dot(x, spread((a * lv).astype(BF16))).astype(BF16) for x, a in zip(db, ba)]
        d = [x + _dot(y, spread(z)) for x, y, z in zip(d, dm, db)]
        yield
    xv = [_dot(jnp.concatenate([x, y], axis=0).astype(BF16), _block_diag(v2[a], v2[b]))
          for x, y, (a, b) in zip(bk, rk, pairs)]
    yield
    lane_lo2 = jnp.concatenate([lane_lo, lane_lo], axis=1)
    heads2_wide = lambda z: jnp.concatenate([jnp.where(lane_lo2, z, 0.0), jnp.where(lane_lo2, 0.0, z)], axis=0)
    zrows = lambda y, ch: heads2_wide(jnp.concatenate([y, bd[ch]], axis=1)).astype(BF16)
    tz = [_dot(x.astype(BF16), _block_diag(zrows(y[:c, :LANES], a), zrows(y[:c, LANES:], b)))
          for x, y, (a, b) in zip(d, xv, pairs)]
    tr, rae, add = [], [], []
    for i, (a, b) in enumerate(pairs):
        for h, ch in enumerate((a, b)):
            t = tz[i][:, 2 * h * LANES:(2 * h + 2) * LANES]
            y = xv[i][:, h * LANES:(h + 1) * LANES]
            aet = jnp.concatenate([jnp.where(row_lo, ae_t[ch], 0.0), jnp.where(row_lo, 0.0, ae_t[ch])], axis=1)
            tr.append(jnp.concatenate([t[:, LANES:], rd[ch]], axis=0).astype(BF16))
            rae.append(jnp.concatenate([ra[i][:, h * LANES:(h + 1) * LANES], aet], axis=0).astype(BF16))
            add.append(jnp.concatenate([t[:, :LANES], y[c:], kv[ch], dcol[ch]], axis=0))
    return tr, rae, add


def _block_diag(x, y):
    z = jnp.zeros_like(x)
    return jnp.concatenate([jnp.concatenate([x, z], axis=1), jnp.concatenate([z, y], axis=1)], axis=0)


def _rwkv_step(tr, rae, add, h):
    c = tr[0].shape[0] // 2
    lane_lo = lax.broadcasted_iota(jnp.int32, (c, LANES), 1) < B_HEAD
    side = lambda f, b: jnp.concatenate([f, b], axis=1)
    lanes = lambda x, d: x[:, d * LANES:(d + 1) * LANES]
    hb = h.astype(BF16)
    p = _dot(side(*tr), _block_diag(hb[:, :LANES], hb[:, LANES:]))
    u = [_stack_heads(add[d][0:c] + lanes(p[:c], d), lane_lo).astype(BF16) for d in range(2)]
    qq = _dot(side(*rae), _block_diag(*u))
    ys = [lanes(p[c:], d) + lanes(qq[:c], d) + add[d][c:2 * c] for d in range(2)]
    kv, dec = slice(2 * c, 2 * c + LANES), slice(2 * c + LANES, 2 * c + 2 * LANES)
    h = side(add[0][dec], add[1][dec]) * h + qq[c:] + side(add[0][kv], add[1][kv])
    return ys, h


N_RWKV_IN = 27
N_RWKV_SCRATCH = 12


def _rwkv_parts(r_ref, k_ref, v_ref, wd_ref, ad_ref, g_ref, mur_ref, muk_ref, muv_ref, muw_ref, mua_ref,
                w0_ref, a0_ref, wup_ref, aup_ref, kk_ref, ka_ref, rk_ref, lng_ref, lnb_ref,
                lo_ref, up_ref, tmask_ref, lvl_ref, head_ref, eye_ref, quarter_ref,
                o_ref, r_s, v_s, kk_s, lw_s, kka_s, km_s, bv_s, y_s, tr_s, rae_s, add_s, cw_s):
    seq = r_ref.shape[0]
    c = B_CHUNK
    n = seq // c
    rb = ROW_BLOCK
    group = B_GROUP
    gc = group * c
    n_it = n // group
    head = head_ref[...]
    hmask = head.astype(F32)
    lane_lo = lax.broadcasted_iota(jnp.int32, (rb, LANES), 1) < B_LORA

    def prologue(t0):
        rows = pl.ds(t0, rb)
        r = _shifted(r_ref, mur_ref[...], t0, rb, seq)
        k = _shifted(k_ref, muk_ref[...], t0, rb, seq)
        v = _shifted(v_ref, muv_ref[...], t0, rb, seq)
        yield
        twd = jnp.tanh(_shifted(wd_ref, muw_ref[...], t0, rb, seq))
        ad = _shifted(ad_ref, mua_ref[...], t0, rb, seq)
        kk = k * kk_ref[...]
        kk = kk / jnp.maximum(jnp.sqrt(_split_dot(kk * kk, head, 0, 2)), 1e-12)
        kmsum = None
        for d in range(2):
            yield
            sel = lane_lo if d == 0 else jnp.logical_not(lane_lo)
            wl = _dot(jnp.where(sel, twd, 0.0).astype(BF16), wup_ref[...].astype(BF16))
            al = _dot(jnp.where(sel, ad, 0.0).astype(BF16), aup_ref[...].astype(BF16))
            lw = -math.exp(-0.5) * _sigmoid(w0_ref[d:d + 1, :] + wl)
            lw_s[d, rows, :] = lw
            cw_s[d, rows, :] = _split_dot((lo_ref if d == 0 else up_ref)[...], lw, 1, 3)
            a = _sigmoid(a0_ref[d:d + 1, :] + al)
            km = k * (1.0 + (a - 1.0) * ka_ref[...])
            kka_s[d, rows, :] = kk * a
            km_s[d, rows, :] = km
            kmsum = km if kmsum is None else kmsum + km
        bonus = _split_dot(r * kmsum * rk_ref[...], head, 0, 2)
        r_s[rows, :] = r
        v_s[rows, :] = v
        kk_s[rows, :] = kk
        bv_s[rows, :] = bonus * v
        yield

    first = [t0 for t0 in range(0, seq, rb) if t0 < gc or t0 >= seq - gc]
    later = [t0 for t0 in range(0, seq, rb) if t0 not in first]
    for t0 in first:
        _interleave(prologue(t0))

    def prologue_rest():
        for t0 in later:
            yield from prologue(t0)

    y_s[...] = jnp.zeros_like(y_s)
    assert lo_ref.shape[0] == rb

    def bases(i):
        return _aligned(i * gc, gc), _aligned((n_it - 1 - i) * gc, gc)

    def prep(d, base):
        rows = pl.ds(base, gc)
        tr, rae, add = yield from _rwkv_prep(
            r_s[rows, :], lw_s[d, rows, :], cw_s[d, rows, :], kk_s[rows, :], kka_s[d, rows, :], v_s[rows, :],
            km_s[d, rows, :], tmask_ref, lvl_ref, eye_ref, quarter_ref, hmask, d == 1, group)
        for j in range(group):
            tr_s[d, j] = tr[j]
            rae_s[d, j] = rae[j]
            add_s[d, j] = add[j]

    def body(i, h):
        base_f, base_b = bases(i)
        yield from _lockstep(prep(0, base_f), prep(1, base_b))
        for j in range(group):
            jb = group - 1 - j
            (yf, yb), h = _rwkv_step((tr_s[0, j], tr_s[1, jb]), (rae_s[0, j], rae_s[1, jb]),
                                     (add_s[0, j], add_s[1, jb]), h)
            y_s[pl.ds(base_f + j * c, c), :] += yf
            y_s[pl.ds(base_b + jb * c, c), :] += yb
            yield
        return h

    def finish(_):
        inv_n = 1.0 / B_HEAD
        for t0 in range(0, seq, rb):
            rows = pl.ds(t0, rb)
            y = y_s[rows, :]
            mu = _split_dot(y, head, 0, 2) * inv_n
            yc = y - mu
            var = _split_dot(yc * yc, head, 0, 2) * inv_n
            yn = yc * lax.rsqrt(var + B_LNX_EPS) * lng_ref[...] + lnb_ref[...]
            g = g_ref[rows, :]
            o_ref[rows, :] = ((yn + bv_s[rows, :]) * (g * _sigmoid(g))).astype(o_ref.dtype)

    return n_it, jnp.zeros((LANES, 2 * LANES), F32), body, finish, prologue_rest()


def _rwkv_operands(proj, mu, w0, a0, wup, aup, k_k, k_a, r_k, lnx_g, lnx_b, seq, bw):
    nb = bw // LANES
    c0 = 4 * nb
    consts = _rwkv_consts(B_CHUNK, B_CUM)
    col = lambda off: pl.BlockSpec((seq, LANES), lambda b, g: (b, off + g))
    colf = lambda off: pl.BlockSpec((seq, LANES), lambda b, g: (b, off))
    par2 = lambda off: pl.BlockSpec((2, LANES), lambda b, g: (0, off + g))
    par2f = lambda off: pl.BlockSpec((2, LANES), lambda b, g: (0, off))
    par1 = pl.BlockSpec((1, LANES), lambda b, g: (0, g))
    up = pl.BlockSpec((2 * B_LORA, LANES), lambda b, g: (0, g))

    def fixed(a):
        return pl.BlockSpec(a.shape, lambda b, g: (0,) * a.ndim)

    row = lambda a: a.reshape(1, bw)
    in_specs = [col(c0), col(c0 + nb), col(c0 + 2 * nb), colf(c0 + 3 * nb), colf(c0 + 3 * nb + 1), col(c0 + 3 * nb + 2),
                par2(0), par2(nb), par2(2 * nb), par2f(3 * nb), par2f(3 * nb + 1),
                par2(0), par2(0), up, up, par1, par1, par1, par1, par1] + [fixed(a) for a in consts]
    operands = (proj, proj, proj, proj, proj, proj, mu, mu, mu, mu, mu, w0, a0,
                wup.reshape(2 * B_LORA, bw), aup.reshape(2 * B_LORA, bw),
                row(k_k), row(k_a), row(r_k), row(lnx_g), row(lnx_b), *consts)
    scratch = [pltpu.VMEM((seq, LANES), F32), pltpu.VMEM((seq, LANES), F32), pltpu.VMEM((seq, LANES), F32),
               pltpu.VMEM((2, seq, LANES), F32), pltpu.VMEM((2, seq, LANES), F32),
               pltpu.VMEM((2, seq, LANES), F32), pltpu.VMEM((seq, LANES), F32),
               pltpu.VMEM((seq, LANES), F32),
               pltpu.VMEM((2, B_GROUP, 2 * B_CHUNK, LANES), BF16),
               pltpu.VMEM((2, B_GROUP, B_CHUNK + LANES, LANES), BF16),
               pltpu.VMEM((2, B_GROUP, 2 * B_CHUNK + 2 * LANES, LANES), F32),
               pltpu.VMEM((2, seq, LANES), F32)]
    assert len(operands) == N_RWKV_IN and len(scratch) == N_RWKV_SCRATCH
    return in_specs, operands, scratch


def _recurrent_body(*refs):
    rwkv_in = refs[:N_RWKV_IN]
    hgrn_in = refs[N_RWKV_IN:N_RWKV_IN + N_HGRN_IN]
    rwkv_out, hgrn_out = refs[N_RWKV_IN + N_HGRN_IN:N_RWKV_IN + N_HGRN_IN + 2]
    scratch = refs[N_RWKV_IN + N_HGRN_IN + 2:]
    trips_b, carry_b, body_b, finish_b, prologue_rest = _rwkv_parts(*rwkv_in, rwkv_out, *scratch[:N_RWKV_SCRATCH])
    trips_c, carry_c, body_c, finish_c = _hgrn_parts(*hgrn_in, hgrn_out, *scratch[N_RWKV_SCRATCH:])
    assert trips_b == trips_c and trips_b >= 2

    def body(i, carry):
        return tuple(_interleave(body_b(i, carry[0]), body_c(i, carry[1])))

    carry_b, carry_c, _ = _interleave(body_b(0, carry_b), body_c(0, carry_c), prologue_rest)
    carry_b, carry_c = lax.fori_loop(1, trips_b, body, (carry_b, carry_c))
    finish_b(carry_b)
    finish_c(carry_c)


def _recurrent(proj, rwkv_params, hgrn_params, bsz, seq, bw):
    t = proj.shape[0]
    nb = bw // LANES
    specs_b, ops_b, scratch_b = _rwkv_operands(proj, *rwkv_params, seq, bw)
    specs_c, ops_c, scratch_c = _hgrn_operands(proj, *hgrn_params, seq, bw)
    out_spec = pl.BlockSpec((seq, LANES), lambda b, g: (b, g))
    out_shape = jax.ShapeDtypeStruct((t, bw), BF16)
    return pl.pallas_call(
        _recurrent_body,
        grid=(bsz, nb),
        in_specs=specs_b + specs_c,
        out_specs=[out_spec, out_spec],
        out_shape=[out_shape, out_shape],
        scratch_shapes=scratch_b + scratch_c,
        compiler_params=_cparams(("parallel", "parallel")),
        name="rwkv7_hgrn2",
    )(*ops_b, *ops_c)


def kernel(x, rel_bias, pre_norm_g, post_norm_g, w_in, w_out, lambda_q1, lambda_k1, lambda_q2, lambda_k2, subln_g, rwkv_shift_mu, rwkv_w0, rwkv_w_up, rwkv_a0, rwkv_a_up, rwkv_k_k, rwkv_k_a, rwkv_r_k, rwkv_lnx_g, rwkv_lnx_b, hgrn_lb_logits, hgrn_norm_g):
    bsz, seq, d = x.shape
    depth = w_in.shape[0]
    bw = d // 2
    assert bw % (2 * LANES) == 0 and seq % (ATT_SUB * ATT_BLOCK) == 0 and seq % (C_GROUP * C_CHUNK) == 0
    assert seq % (B_GROUP * B_CHUNK) == 0
    assert w_in.shape[2] == 13 * bw + 4 * B_LORA
    x2 = x.reshape(bsz * seq, d)
    lb = jax.nn.softmax(hgrn_lb_logits.astype(F32), axis=1)
    lb = jnp.cumsum(lb, axis=1) - lb[:, :1]
    band = _bias_band(rel_bias, ATT_BLOCK)
    w_in_b, w_out_b = w_in.astype(BF16), w_out.astype(BF16)
    for l in range(depth):
        proj = _proj(x2, pre_norm_g[l], w_in_b, l)
        lam_p = jnp.stack([lambda_q1[l], lambda_k1[l], lambda_q2[l], lambda_k2[l]]).astype(F32)
        ya = _attn(proj, lam_p, band, subln_g[l], bsz, seq, bw, l)
        yb, yc = _recurrent(proj, (rwkv_shift_mu[l], rwkv_w0[l], rwkv_a0[l], rwkv_w_up[l], rwkv_a_up[l], rwkv_k_k[l],
                                   rwkv_k_a[l], rwkv_r_k[l], rwkv_lnx_g[l], rwkv_lnx_b[l]),
                            (lb[:, l], hgrn_norm_g[l]), bsz, seq, bw)
        x2 = _out(ya, yb, yc, w_out_b, l, post_norm_g[l], x2)
    return x2.reshape(bsz, seq, d)
```
